```python
import math
import jax, jax.numpy as jnp
from jax import lax
import numpy as np

D_MODEL = 4096
BATCH = 4
SEQ = 2048
DEPTH = 4
DEC_BATCH = 8
DEC_SEQ = 2048
PAST_LEN = 128

HEAD_DIM = 128
N_MIXERS = 4
GRID_W = 64
ROPE_THETA = 10000.0
Q_BLOCK = 128
EPS = 1e-6
D_FF = 4 * D_MODEL
A_HEADS = D_MODEL // HEAD_DIM
A_KV_HEADS = A_HEADS // 4
A_IN = (A_HEADS + 2 * A_KV_HEADS) * HEAD_DIM
B_CONFIGS = ((128, 1), (512, 4), (2048, 16))
B_N_GROUPS = len(B_CONFIGS)
B_HEADS = A_HEADS // 2
B_KV_HEADS = B_HEADS // 4
B_PER_GROUP = (B_HEADS + 2 * B_KV_HEADS) * HEAD_DIM
B_IN = B_N_GROUPS * B_PER_GROUP
C_HEADS = D_MODEL // (2 * HEAD_DIM)
C_IN = 3 * C_HEADS * 2 * HEAD_DIM
D_HEADS = A_HEADS
D_KV_HEADS = A_KV_HEADS
D_IN = (D_HEADS + 2 * D_KV_HEADS) * HEAD_DIM
NA_ROWS_MAX = 8
NA_COLS = 16
N_PER_MIXER = tuple((DEPTH - m + N_MIXERS - 1) // N_MIXERS for m in range(N_MIXERS))

kernel_name = "hybrid_bidir_encoder_interleaved"


def rms_norm(x, g):
    xf = x.astype(jnp.float32)
    y = xf * lax.rsqrt(jnp.mean(xf * xf, axis=-1, keepdims=True) + EPS)
    return (y * g.astype(jnp.float32)).astype(x.dtype)


def rope_angles(pos, dim):
    inv = ROPE_THETA ** (-jnp.arange(0, dim, 2, dtype=jnp.float32) / dim)
    ang = pos.astype(jnp.float32)[:, None] * inv[None, :]
    ang = jnp.concatenate([ang, ang], axis=-1)
    return jnp.cos(ang), jnp.sin(ang)


def apply_rope(x, cos, sin):
    shp = (1, cos.shape[0]) + (1,) * (x.ndim - 3) + (cos.shape[1],)
    c, s = cos.reshape(shp), sin.reshape(shp)
    half = x.shape[-1] // 2
    xf = x.astype(jnp.float32)
    rot = jnp.concatenate([-xf[..., half:], xf[..., :half]], axis=-1)
    return (xf * c + rot * s).astype(x.dtype)


def apply_axial_rope(x, rows, cols):
    half = x.shape[-1] // 2
    cr, sr = rope_angles(rows, half)
    cc, sc = rope_angles(cols, half)
    return jnp.concatenate([apply_rope(x[..., :half], cr, sr),
                            apply_rope(x[..., half:], cc, sc)], axis=-1)


def blocked_gqa_attention(q, k, v):
    b, s, hkv, g, hd = q.shape
    nb = s // Q_BLOCK
    qb = jnp.moveaxis(q.reshape(b, nb, Q_BLOCK, hkv, g, hd), 1, 0)
    scale = hd ** -0.5

    def one(qblk):
        sc = jnp.einsum('bqhgd,bkhd->bhgqk', qblk, k, preferred_element_type=jnp.float32) * scale
        p = jax.nn.softmax(sc, axis=-1)
        return jnp.einsum('bhgqk,bkhd->bqhgd', p.astype(v.dtype), v)

    out = lax.map(one, qb)
    return jnp.moveaxis(out, 0, 1).reshape(b, s, hkv, g, hd)


def mixer_axial_gqa(h, w_in, w_out, q_gain, k_gain):
    b, s, _ = h.shape
    g = A_HEADS // A_KV_HEADS
    nq, nkv = A_HEADS * HEAD_DIM, A_KV_HEADS * HEAD_DIM
    proj = h @ w_in
    q = rms_norm(proj[..., :nq].reshape(b, s, A_KV_HEADS, g, HEAD_DIM), q_gain)
    k = rms_norm(proj[..., nq:nq + nkv].reshape(b, s, A_KV_HEADS, HEAD_DIM), k_gain)
    v = proj[..., nq + nkv:].reshape(b, s, A_KV_HEADS, HEAD_DIM)
    t = jnp.arange(s)
    rows, cols = t // GRID_W, t % GRID_W
    q = apply_axial_rope(q, rows, cols)
    k = apply_axial_rope(k, rows, cols)
    o = blocked_gqa_attention(q, k, v)
    return o.reshape(b, s, nq) @ w_out


def dilated_group_attention(q, k, v, window, dil):
    b, s, hkv, g, hd = q.shape
    nb = s // Q_BLOCK
    half = window // (2 * dil)
    offs = dil * jnp.arange(-half, half + 1)
    qb = jnp.moveaxis(q.reshape(b, nb, Q_BLOCK, hkv, g, hd), 1, 0)
    starts = jnp.arange(nb) * Q_BLOCK
    scale = hd ** -0.5

    def one(args):
        qblk, st = args
        tq = st + jnp.arange(Q_BLOCK)
        idx = tq[:, None] + offs[None, :]
        valid = (idx >= 0) & (idx < s)
        idx = jnp.clip(idx, 0, s - 1)
        kg = k[:, idx]
        vg = v[:, idx]
        sc = jnp.einsum('bqhgd,bqjhd->bhgqj', qblk, kg, preferred_element_type=jnp.float32) * scale
        sc = jnp.where(valid[None, None, None], sc, -jnp.inf)
        lse = jax.nn.logsumexp(sc, axis=-1)
        p = jnp.exp(sc - lse[..., None])
        o = jnp.einsum('bhgqj,bqjhd->bqhgd', p.astype(vg.dtype), vg)
        return o, jnp.transpose(lse, (0, 3, 1, 2))

    o, lse = lax.map(one, (qb, starts))
    o = jnp.moveaxis(o, 0, 1).reshape(b, s, hkv, g, hd)
    lse = jnp.moveaxis(lse, 0, 1).reshape(b, s, hkv, g)
    return o, lse


def mixer_dilated(h, w_in, w_out, q_gain, k_gain):
    b, s, _ = h.shape
    g = B_HEADS // B_KV_HEADS
    nq, nkv = B_HEADS * HEAD_DIM, B_KV_HEADS * HEAD_DIM
    proj = (h @ w_in).reshape(b, s, B_N_GROUPS, B_PER_GROUP)
    cos, sin = rope_angles(jnp.arange(s), HEAD_DIM)
    outs, lses = [], []
    for gi, (window, dil) in enumerate(B_CONFIGS):
        pg = proj[:, :, gi]
        q = rms_norm(pg[..., :nq].reshape(b, s, B_KV_HEADS, g, HEAD_DIM), q_gain[gi])
        k = rms_norm(pg[..., nq:nq + nkv].reshape(b, s, B_KV_HEADS, HEAD_DIM), k_gain[gi])
        v = pg[..., nq + nkv:].reshape(b, s, B_KV_HEADS, HEAD_DIM)
        q = apply_rope(q, cos, sin)
        k = apply_rope(k, cos, sin)
        o, lse = dilated_group_attention(q, k, v, window, dil)
        outs.append(o)
        lses.append(lse)
    alpha = jax.nn.softmax(jnp.stack(lses, axis=0), axis=0)
    o = jnp.sum(alpha[..., None] * jnp.stack(outs, axis=0).astype(jnp.float32), axis=0)
    return o.astype(h.dtype).reshape(b, s, nq) @ w_out


def mixer_differential(h, w_in, w_out, q_gain, k_gain, lam_params, subln, lambda_init):
    b, s, _ = h.shape
    hd = HEAD_DIM
    nqk = C_HEADS * 2 * hd
    proj = h @ w_in
    q = rms_norm(proj[..., :nqk].reshape(b, s, C_HEADS, 2, hd), q_gain)
    k = rms_norm(proj[..., nqk:2 * nqk].reshape(b, s, C_HEADS, 2, hd), k_gain)
    v = proj[..., 2 * nqk:].reshape(b, s, C_HEADS, 2 * hd)
    cos, sin = rope_angles(jnp.arange(s), hd)
    q = apply_rope(q, cos, sin)
    k = apply_rope(k, cos, sin)
    lp = lam_params.astype(jnp.float32)
    lam = jnp.exp(jnp.sum(lp[0] * lp[1])) - jnp.exp(jnp.sum(lp[2] * lp[3])) + lambda_init
    nb = s // Q_BLOCK
    qb = jnp.moveaxis(q.reshape(b, nb, Q_BLOCK, C_HEADS, 2, hd), 1, 0)
    scale = hd ** -0.5

    def one(qblk):
        sc = jnp.einsum('bqhmd,bkhmd->bhmqk', qblk, k, preferred_element_type=jnp.float32) * scale
        p = jax.nn.softmax(sc, axis=-1)
        a = p[:, :, 0] - lam * p[:, :, 1]
        return jnp.einsum('bhqk,bkhe->bqhe', a.astype(v.dtype), v)

    o = jnp.moveaxis(lax.map(one, qb), 0, 1).reshape(b, s, C_HEADS, 2 * hd)
    o = rms_norm(o, subln) * (1.0 - lambda_init)
    return o.reshape(b, s, C_HEADS * 2 * hd) @ w_out


def mixer_neighbourhood(h, w_in, w_out, q_gain, k_gain, rpb):
    b, s, _ = h.shape
    n_rows = s // GRID_W
    kh, kw = min(NA_ROWS_MAX, n_rows), NA_COLS
    g = D_HEADS // D_KV_HEADS
    nq, nkv = D_HEADS * HEAD_DIM, D_KV_HEADS * HEAD_DIM
    proj = h @ w_in
    q = rms_norm(proj[..., :nq].reshape(b, s, D_KV_HEADS, g, HEAD_DIM), q_gain)
    k = rms_norm(proj[..., nq:nq + nkv].reshape(b, s, D_KV_HEADS, HEAD_DIM), k_gain)
    v = proj[..., nq + nkv:].reshape(b, s, D_KV_HEADS, HEAD_DIM)
    qr = jnp.moveaxis(q.reshape(b, n_rows, GRID_W, D_KV_HEADS, g, HEAD_DIM), 1, 0)
    cols = jnp.arange(GRID_W)
    col_start = jnp.clip(cols - kw // 2, 0, GRID_W - kw)
    col_idx = col_start[:, None] + jnp.arange(kw)[None, :]
    col_off = col_idx - cols[:, None]
    ii = jnp.arange(kh)
    scale = HEAD_DIM ** -0.5

    def one(args):
        qrow, r = args
        row_start = jnp.clip(r - kh // 2, 0, n_rows - kh)
        key_rows = row_start + ii
        idx = (key_rows[None, :, None] * GRID_W + col_idx[:, None, :]).reshape(GRID_W, kh * kw)
        kg = k[:, idx]
        vg = v[:, idx]
        ro = (key_rows - r)[None, :, None] + (NA_ROWS_MAX - 1)
        co = col_off[:, None, :] + (NA_COLS - 1)
        bias = rpb[:, ro, co].reshape(D_KV_HEADS, g, GRID_W, kh * kw).astype(jnp.float32)
        sc = jnp.einsum('bqhgd,bqjhd->bhgqj', qrow, kg, preferred_element_type=jnp.float32) * scale
        p = jax.nn.softmax(sc + bias[None], axis=-1)
        return jnp.einsum('bhgqj,bqjhd->bqhgd', p.astype(vg.dtype), vg)

    o = lax.map(one, (qr, jnp.arange(n_rows)))
    o = jnp.moveaxis(o, 0, 1).reshape(b, s, nq)
    return o @ w_out


def squared_relu_mlp(h, w_up, w_down):
    u = jax.nn.relu(h @ w_up)
    return (u * u) @ w_down


def trunk(x, ln_mix, ln_ffn, w_up, w_down,
          a_w_in, a_w_out, a_q_gain, a_k_gain,
          b_w_in, b_w_out, b_q_gain, b_k_gain,
          c_w_in, c_w_out, c_q_gain, c_k_gain, c_lambda, c_subln,
          d_w_in, d_w_out, d_q_gain, d_k_gain, d_rpb):
    for i in range(DEPTH):
        m, j = i % N_MIXERS, i // N_MIXERS
        hn = rms_norm(x, ln_mix[i])
        if m == 0:
            y = mixer_axial_gqa(hn, a_w_in[j], a_w_out[j], a_q_gain[j], a_k_gain[j])
        elif m == 1:
            y = mixer_dilated(hn, b_w_in[j], b_w_out[j], b_q_gain[j], b_k_gain[j])
        elif m == 2:
            lambda_init = 0.8 - 0.6 * math.exp(-0.3 * i)
            y = mixer_differential(hn, c_w_in[j], c_w_out[j], c_q_gain[j], c_k_gain[j],
                                   c_lambda[j], c_subln[j], lambda_init)
        else:
            y = mixer_neighbourhood(hn, d_w_in[j], d_w_out[j], d_q_gain[j], d_k_gain[j], d_rpb[j])
        x = x + y
        x = x + squared_relu_mlp(rms_norm(x, ln_ffn[i]), w_up[i], w_down[i])
    return x


def setup_inputs(seed: int = 0) -> dict:
    key = jax.random.key(seed)
    ks = iter(jax.random.split(key, 32))

    def nrm(shape, scale):
        return jax.random.normal(next(ks), shape, jnp.float32) * scale

    def gain(shape):
        return 1.0 + nrm(shape, 0.05)

    nA, nB, nC, nD = N_PER_MIXER
    return {
        "x_prompt": nrm((BATCH, SEQ, D_MODEL), 1.0),
        "x_sample": nrm((DEC_BATCH, DEC_SEQ, D_MODEL), 1.0),
        "ln_mix": gain((DEPTH, D_MODEL)),
        "ln_ffn": gain((DEPTH, D_MODEL)),
        "w_up": nrm((DEPTH, D_MODEL, D_FF), D_MODEL ** -0.5),
        "w_down": nrm((DEPTH, D_FF, D_MODEL), D_FF ** -0.5),
        "a_w_in": nrm((nA, D_MODEL, A_IN), D_MODEL ** -0.5),
        "a_w_out": nrm((nA, A_HEADS * HEAD_DIM, D_MODEL), (A_HEADS * HEAD_DIM) ** -0.5),
        "a_q_gain": gain((nA, HEAD_DIM)),
        "a_k_gain": gain((nA, HEAD_DIM)),
        "b_w_in": nrm((nB, D_MODEL, B_IN), D_MODEL ** -0.5),
        "b_w_out": nrm((nB, B_HEADS * HEAD_DIM, D_MODEL), (B_HEADS * HEAD_DIM) ** -0.5),
        "b_q_gain": gain((nB, B_N_GROUPS, HEAD_DIM)),
        "b_k_gain": gain((nB, B_N_GROUPS, HEAD_DIM)),
        "c_w_in": nrm((nC, D_MODEL, C_IN), D_MODEL ** -0.5),
        "c_w_out": nrm((nC, C_HEADS * 2 * HEAD_DIM, D_MODEL), (C_HEADS * 2 * HEAD_DIM) ** -0.5),
        "c_q_gain": gain((nC, HEAD_DIM)),
        "c_k_gain": gain((nC, HEAD_DIM)),
        "c_lambda": nrm((nC, 4, HEAD_DIM), 0.1),
        "c_subln": gain((nC, 2 * HEAD_DIM)),
        "d_w_in": nrm((nD, D_MODEL, D_IN), D_MODEL ** -0.5),
        "d_w_out": nrm((nD, D_HEADS * HEAD_DIM, D_MODEL), (D_HEADS * HEAD_DIM) ** -0.5),
        "d_q_gain": gain((nD, HEAD_DIM)),
        "d_k_gain": gain((nD, HEAD_DIM)),
        "d_rpb": nrm((nD, D_HEADS, 2 * NA_ROWS_MAX - 1, 2 * NA_COLS - 1), 0.2),
    }


def reference(x_prompt, x_sample, ln_mix, ln_ffn, w_up, w_down,
              a_w_in, a_w_out, a_q_gain, a_k_gain,
              b_w_in, b_w_out, b_q_gain, b_k_gain,
              c_w_in, c_w_out, c_q_gain, c_k_gain, c_lambda, c_subln,
              d_w_in, d_w_out, d_q_gain, d_k_gain, d_rpb):
    y_prompt = trunk(x_prompt, ln_mix, ln_ffn, w_up, w_down,
                     a_w_in, a_w_out, a_q_gain, a_k_gain,
                     b_w_in, b_w_out, b_q_gain, b_k_gain,
                     c_w_in, c_w_out, c_q_gain, c_k_gain, c_lambda, c_subln,
                     d_w_in, d_w_out, d_q_gain, d_k_gain, d_rpb)
    y_sample = trunk(x_sample, ln_mix, ln_ffn, w_up, w_down,
                     a_w_in, a_w_out, a_q_gain, a_k_gain,
                     b_w_in, b_w_out, b_q_gain, b_k_gain,
                     c_w_in, c_w_out, c_q_gain, c_k_gain, c_lambda, c_subln,
                     d_w_in, d_w_out, d_q_gain, d_k_gain, d_rpb)
    return (y_prompt, y_sample)
```

```python
import functools
import math

import jax
import jax.numpy as jnp
from jax import lax
from jax.experimental import pallas as pl
from jax.experimental.pallas import tpu as pltpu

F32 = jnp.float32
BF16 = jnp.bfloat16

HEAD_DIM = 128
GRID_W = 64
ROPE_THETA = 10000.0
EPS = 1e-6
Q_SCALE = HEAD_DIM ** -0.5
N_MIXERS = 4
A_HEADS, A_KV_HEADS = 32, 8
B_CONFIGS = ((128, 1), (512, 4), (2048, 16))
B_HEADS, B_KV_HEADS = 16, 4
C_HEADS = 16
D_HEADS, D_KV_HEADS = 32, 8
NA_ROWS, NA_COLS = 8, 16
MASKED = -1e30

LANES = 128
V7X_VMEM_BYTES = 64 * 1024 * 1024
VMEM_LIMIT_BYTES = V7X_VMEM_BYTES - 8 * 1024 * 1024


def _tile(dim, pref, quantum):
    if dim <= pref:
        return dim
    t = (pref // quantum) * quantum
    while dim % t:
        t -= quantum
    return t


def _params(*semantics):
    return pltpu.CompilerParams(dimension_semantics=semantics, vmem_limit_bytes=VMEM_LIMIT_BYTES)


def _rmsnorm_kernel(x_ref, g_ref, o_ref):
    x = x_ref[...]
    ms = jnp.mean(x * x, axis=-1, keepdims=True)
    o_ref[...] = (x * lax.rsqrt(ms + EPS) * g_ref[...]).astype(o_ref.dtype)


def rmsnorm(x, gain):
    t, d = x.shape
    tr = _tile(t, 512, 8)
    return pl.pallas_call(
        _rmsnorm_kernel,
        out_shape=jax.ShapeDtypeStruct((t, d), BF16),
        grid=(t // tr,),
        in_specs=[pl.BlockSpec((tr, d), lambda i: (i, 0)), pl.BlockSpec((1, d), lambda i: (0, 0))],
        out_specs=pl.BlockSpec((tr, d), lambda i: (i, 0)),
        compiler_params=_params("parallel"),
        name="rmsnorm",
    )(x, gain.reshape(1, d).astype(F32))


def _matmul_kernel(a_ref, w_ref, *refs, nk, act, has_res):
    if has_res:
        res_ref, o_ref, *scratch = refs
    else:
        res_ref = None
        o_ref, *scratch = refs

    def finish(acc):
        if act == "relu2":
            r = jnp.maximum(acc, 0.0)
            acc = r * r
        o_ref[...] = acc.astype(o_ref.dtype)

    part = jnp.dot(a_ref[...], w_ref[...], preferred_element_type=F32)
    if nk == 1:
        finish(part + res_ref[...] if has_res else part)
        return

    (acc_ref,) = scratch
    k = pl.program_id(2)

    @pl.when(k == 0)
    def _():
        acc_ref[...] = res_ref[...] if has_res else jnp.zeros_like(acc_ref)

    acc_ref[...] += part

    @pl.when(k == nk - 1)
    def _():
        finish(acc_ref[...])


def matmul(a, w, *, out_dtype, act=None, res=None, tm=1024, tn=1024, tk=2048, name="matmul"):
    m, kdim = a.shape
    _, n = w.shape
    tm, tn, tk = _tile(m, tm, 8), _tile(n, tn, LANES), _tile(kdim, tk, LANES)
    nk = kdim // tk
    in_specs = [
        pl.BlockSpec((tm, tk), lambda i, j, k: (i, k)),
        pl.BlockSpec((tk, tn), lambda i, j, k: (k, j)),
    ]
    args = [a, w]
    if res is not None:
        in_specs.append(pl.BlockSpec((tm, tn), lambda i, j, k: (i, j)))
        args.append(res)
    return pl.pallas_call(
        functools.partial(_matmul_kernel, nk=nk, act=act, has_res=res is not None),
        out_shape=jax.ShapeDtypeStruct((m, n), out_dtype),
        grid=(m // tm, n // tn, nk),
        in_specs=in_specs,
        out_specs=pl.BlockSpec((tm, tn), lambda i, j, k: (i, j)),
        scratch_shapes=[pltpu.VMEM((tm, tn), F32)] if nk > 1 else [],
        compiler_params=_params("parallel", "parallel", "arbitrary"),
        name=name,
    )(*args)


def rope_tables(pos, dim):
    inv = ROPE_THETA ** (-jnp.arange(0, dim, 2, dtype=F32) / dim)
    ang = pos.astype(F32)[:, None] * inv[None, :]
    ang = jnp.concatenate([ang, ang], axis=-1)
    return jnp.cos(ang), jnp.sin(ang)


def full_rope_tables(s):
    cos, sin = rope_tables(jnp.arange(s), HEAD_DIM)
    lane = jnp.arange(HEAD_DIM)
    sin_signed = jnp.where(lane < HEAD_DIM // 2, -sin, sin)
    return ((HEAD_DIM // 2,), (cos, sin_signed))


def axial_rope_tables(s):
    half = HEAD_DIM // 2
    t = jnp.arange(s)
    cr, sr = rope_tables(t // GRID_W, half)
    cc, sc = rope_tables(t % GRID_W, half)
    cos = jnp.concatenate([cr, cc], axis=-1)
    sin = jnp.concatenate([sr, sc], axis=-1)
    low = (jnp.arange(HEAD_DIM) % half) < half // 2
    return ((HEAD_DIM - half // 2, half // 2),
            (cos, jnp.where(low, -sin, 0.0), jnp.where(low, 0.0, sin)))


def _prep_kernel(x_ref, g_ref, *refs, shifts, scale, heads):
    *tab_refs, o_ref = refs
    g = g_ref[pl.ds(pl.program_id(2), 1), :]
    for h in range(heads):
        cols = slice(h * HEAD_DIM, (h + 1) * HEAD_DIM)
        x = x_ref[0, :, cols].astype(F32)
        ms = jnp.mean(x * x, axis=-1, keepdims=True)
        y = x * lax.rsqrt(ms + EPS) * g
        if shifts:
            r = y * tab_refs[0][...]
            for sh, t_ref in zip(shifts, tab_refs[1:]):
                r = r + pltpu.roll(y, sh, 1) * t_ref[...]
            y = r
        if scale != 1.0:
            y = y * scale
        o_ref[0, :, cols] = y.astype(o_ref.dtype)


def prep_heads(proj, gains, src_block, n_blocks, *, rope=None, scale=1.0, heads=4, name="prep"):
    b, s, _ = proj.shape
    w = heads * HEAD_DIM
    ts = _tile(s, 1024, 16)
    shifts, tables = rope if rope is not None else ((), ())
    in_specs = [
        pl.BlockSpec((1, ts, w), lambda bi, si, j: (bi, si, src_block(j))),
        pl.BlockSpec((n_blocks, HEAD_DIM), lambda bi, si, j: (0, 0)),
    ] + [pl.BlockSpec((ts, HEAD_DIM), lambda bi, si, j: (si, 0)) for _ in tables]
    return pl.pallas_call(
        functools.partial(_prep_kernel, shifts=shifts, scale=scale, heads=heads),
        out_shape=jax.ShapeDtypeStruct((b, s, n_blocks * w), BF16),
        grid=(b, s // ts, n_blocks),
        in_specs=in_specs,
        out_specs=pl.BlockSpec((1, ts, w), lambda bi, si, j: (bi, si, j)),
        compiler_params=_params("parallel", "parallel", "arbitrary"),
        name=name,
    )(proj, gains.astype(F32), *[t.astype(F32) for t in tables])


def _stack_heads(x, n):
    return jnp.concatenate([x[:, h * HEAD_DIM:(h + 1) * HEAD_DIM] for h in range(n)], axis=0)


def _unstack_heads(x, n):
    rows = x.shape[0] // n
    return jnp.concatenate([x[h * rows:(h + 1) * rows] for h in range(n)], axis=1)


def _qk(q, k):
    return lax.dot_general(q, k, (((1,), (1,)), ((), ())), preferred_element_type=F32)


def _dense_gqa_kernel(q_ref, k_ref, v_ref, o_ref, *, group):
    q = _stack_heads(q_ref[0], group)
    s = _qk(q, k_ref[0])
    m = jnp.max(s, axis=-1, keepdims=True)
    p = jnp.exp(s - m)
    l = jnp.sum(p, axis=-1, keepdims=True)
    o = jnp.dot(p.astype(BF16), v_ref[0], preferred_element_type=F32) / l
    o_ref[0] = _unstack_heads(o, group).astype(o_ref.dtype)


def dense_gqa(qp, kp, proj, v_block0, *, kv_heads, group, tq=128):
    b, s, _ = qp.shape
    tq = _tile(s, tq, 16)
    w = group * HEAD_DIM
    return pl.pallas_call(
        functools.partial(_dense_gqa_kernel, group=group),
        out_shape=jax.ShapeDtypeStruct(qp.shape, BF16),
        grid=(b, kv_heads, s // tq),
        in_specs=[
            pl.BlockSpec((1, tq, w), lambda bi, g, i: (bi, i, g)),
            pl.BlockSpec((1, s, HEAD_DIM), lambda bi, g, i: (bi, 0, g)),
            pl.BlockSpec((1, s, HEAD_DIM), lambda bi, g, i: (bi, 0, v_block0 + g)),
        ],
        out_specs=pl.BlockSpec((1, tq, w), lambda bi, g, i: (bi, i, g)),
        compiler_params=_params("parallel", "parallel", "arbitrary"),
        name="dense_gqa",
    )(qp, kp, proj)


def _dilated_kernel(*refs, configs, seq, group):
    o_ref = refs[-1]
    i = pl.program_id(2)
    tq = o_ref.shape[1]
    outs, lses = [], []
    for gi, (window, dil) in enumerate(configs):
        q_ref, k_ref, v_ref = refs[3 * gi:3 * gi + 3]
        reach = (window // (2 * dil)) * dil
        pad = -(-reach // tq) * tq
        wlen = min(seq, tq + 2 * pad)
        start = pl.multiple_of(jnp.clip(i * tq - pad, 0, seq - wlen), tq)
        q = _stack_heads(q_ref[0], group)
        s = _qk(q, k_ref[0, pl.ds(start, wlen), :])
        delta = (start + lax.broadcasted_iota(jnp.int32, (tq, wlen), 1)
                 - i * tq - lax.broadcasted_iota(jnp.int32, (tq, wlen), 0))
        ok = (jnp.abs(delta) <= reach) & ((delta & (dil - 1)) == 0)
        s = jnp.where(ok[None], s.reshape(group, tq, wlen), MASKED).reshape(group * tq, wlen)
        m = jnp.max(s, axis=-1, keepdims=True)
        p = jnp.exp(s - m)
        l = jnp.sum(p, axis=-1, keepdims=True)
        o = jnp.dot(p.astype(BF16), v_ref[0, pl.ds(start, wlen), :], preferred_element_type=F32)
        outs.append(o / l)
        lses.append(m + jnp.log(l))
    top = functools.reduce(jnp.maximum, lses)
    ws = [jnp.exp(x - top) for x in lses]
    o = sum(w * x for w, x in zip(ws, outs)) / sum(ws)
    o_ref[0] = _unstack_heads(o, group).astype(o_ref.dtype)


def dilated_attention(qp, kp, proj, *, configs=B_CONFIGS, tq=128):
    b, s, _ = proj.shape
    for _, dil in configs:
        assert dil & (dil - 1) == 0, "the residue test uses a power-of-two dilation"
    group = B_HEADS // B_KV_HEADS
    w = group * HEAD_DIM
    per_group_blocks = B_HEADS + 2 * B_KV_HEADS
    in_specs, args = [], []
    for gi in range(len(configs)):
        in_specs += [
            pl.BlockSpec((1, tq, w), lambda bi, h, i, gi=gi: (bi, i, gi * B_KV_HEADS + h)),
            pl.BlockSpec((1, s, HEAD_DIM), lambda bi, h, i, gi=gi: (bi, 0, gi * B_KV_HEADS + h)),
            pl.BlockSpec((1, s, HEAD_DIM),
                         lambda bi, h, i, gi=gi: (bi, 0, gi * per_group_blocks + B_HEADS + B_KV_HEADS + h)),
        ]
        args += [qp, kp, proj]
    return pl.pallas_call(
        functools.partial(_dilated_kernel, configs=configs, seq=s, group=group),
        out_shape=jax.ShapeDtypeStruct((b, s, B_HEADS * HEAD_DIM), BF16),
        grid=(b, B_KV_HEADS, s // tq),
        in_specs=in_specs,
        out_specs=pl.BlockSpec((1, tq, w), lambda bi, h, i: (bi, i, h)),
        compiler_params=_params("parallel", "parallel", "arbitrary"),
        name="dilated_attention",
    )(*args)


def _differential_kernel(q_ref, k_ref, v_ref, lam_ref, sub_ref, o_ref, *, lambda_init):
    lp = lam_ref[...]
    lam = (jnp.exp(jnp.sum(lp[0:1] * lp[1:2], axis=-1, keepdims=True))
           - jnp.exp(jnp.sum(lp[2:3] * lp[3:4], axis=-1, keepdims=True)) + lambda_init)
    q, k = q_ref[0], k_ref[0]
    probs = []
    for mi in range(2):
        cols = slice(mi * HEAD_DIM, (mi + 1) * HEAD_DIM)
        s = _qk(q[:, cols], k[:, cols])
        m = jnp.max(s, axis=-1, keepdims=True)
        p = jnp.exp(s - m)
        probs.append((p, jnp.sum(p, axis=-1, keepdims=True)))
    (p0, l0), (p1, l1) = probs
    a = p0 * (1.0 / l0) - p1 * (lam / l1)
    o = jnp.dot(a.astype(BF16), v_ref[0], preferred_element_type=F32)
    ms = jnp.mean(o * o, axis=-1, keepdims=True)
    o = o * lax.rsqrt(ms + EPS) * sub_ref[...] * (1.0 - lambda_init)
    o_ref[0] = o.astype(o_ref.dtype)


def differential_attention(qp, kp, proj, lam_params, subln, lambda_init, *, tq=256):
    b, s, _ = qp.shape
    tq = _tile(s, tq, 16)
    w = 2 * HEAD_DIM
    return pl.pallas_call(
        functools.partial(_differential_kernel, lambda_init=lambda_init),
        out_shape=jax.ShapeDtypeStruct(qp.shape, BF16),
        grid=(b, C_HEADS, s // tq),
        in_specs=[
            pl.BlockSpec((1, tq, w), lambda bi, h, i: (bi, i, h)),
            pl.BlockSpec((1, s, w), lambda bi, h, i: (bi, 0, h)),
            pl.BlockSpec((1, s, w), lambda bi, h, i: (bi, 0, 2 * C_HEADS + h)),
            pl.BlockSpec((4, HEAD_DIM), lambda bi, h, i: (0, 0)),
            pl.BlockSpec((1, w), lambda bi, h, i: (0, 0)),
        ],
        out_specs=pl.BlockSpec((1, tq, w), lambda bi, h, i: (bi, i, h)),
        compiler_params=_params("parallel", "parallel", "arbitrary"),
        name="differential_attention",
    )(qp, kp, proj, lam_params.astype(F32), subln.reshape(1, w).astype(F32))


def _na_bias_kernel(rpb_ref, o_ref):
    rows = lax.broadcasted_iota(jnp.int32, (GRID_W, LANES), 0)
    lane = lax.broadcasted_iota(jnp.int32, (GRID_W, LANES), 1)
    n_rel = 2 * NA_ROWS - 1
    toeplitz = []
    for rho in range(n_rel):
        z = jnp.broadcast_to(rpb_ref[0, rho:rho + 1, :], (GRID_W, LANES))
        z = pltpu.roll(z, LANES - (NA_COLS - 1), 1)
        bit = 1
        while bit < GRID_W:
            z = jnp.where((rows & bit) != 0, pltpu.roll(z, bit, 1), z)
            bit *= 2
        toeplitz.append(z)
    kc = lane & (GRID_W - 1)
    first = jnp.clip(rows - NA_COLS // 2, 0, GRID_W - NA_COLS)
    inside = (kc >= first) & (kc < first + NA_COLS)
    for d in range(NA_ROWS):
        for pair in range(NA_ROWS // 2):
            even = toeplitz[d + 2 * pair]
            odd = pltpu.roll(toeplitz[d + 2 * pair + 1], GRID_W, 1)
            tile = jnp.where(lane < GRID_W, even, odd)
            o_ref[d, 0, :, pair * LANES:(pair + 1) * LANES] = jnp.where(inside, tile, MASKED)


def na_bias_table(rpb):
    h = rpb.shape[0]
    padded = jnp.zeros((h, 2 * NA_ROWS, LANES), F32).at[:, :2 * NA_ROWS - 1, :2 * NA_COLS - 1].set(rpb.astype(F32))
    return pl.pallas_call(
        _na_bias_kernel,
        out_shape=jax.ShapeDtypeStruct((NA_ROWS, h, GRID_W, NA_ROWS * GRID_W), F32),
        grid=(h,),
        in_specs=[pl.BlockSpec((1, 2 * NA_ROWS, LANES), lambda i: (i, 0, 0))],
        out_specs=pl.BlockSpec((NA_ROWS, 1, GRID_W, NA_ROWS * GRID_W), lambda i: (0, i, 0, 0)),
        compiler_params=_params("parallel"),
        name="na_bias_table",
    )(padded)


def _neighbourhood_kernel(q_ref, k_ref, v_ref, b_ref, o_ref, *, n_rows, group):
    win = NA_ROWS * GRID_W

    def row(r, carry):
        first = jnp.clip(r - NA_ROWS // 2, 0, n_rows - NA_ROWS)
        q0 = pl.multiple_of(r * GRID_W, GRID_W)
        k0 = pl.multiple_of(first * GRID_W, GRID_W)
        q = _stack_heads(q_ref[0, pl.ds(q0, GRID_W), :], group)
        s = _qk(q, k_ref[0, pl.ds(k0, win), :])
        s = (s.reshape(group, GRID_W, win) + b_ref[first - r + NA_ROWS - 1]).reshape(group * GRID_W, win)
        m = jnp.max(s, axis=-1, keepdims=True)
        p = jnp.exp(s - m)
        l = jnp.sum(p, axis=-1, keepdims=True)
        o = jnp.dot(p.astype(BF16), v_ref[0, pl.ds(k0, win), :], preferred_element_type=F32) / l
        o_ref[0, pl.ds(q0, GRID_W), :] = _unstack_heads(o, group).astype(o_ref.dtype)
        return carry

    lax.fori_loop(0, n_rows, row, 0)


def neighbourhood_attention(qp, kp, proj, v_block0, bias, *, kv_heads, group):
    b, s, _ = qp.shape
    n_rows = s // GRID_W
    assert n_rows >= NA_ROWS, "the key window is NA_ROWS grid rows tall"
    w = group * HEAD_DIM
    return pl.pallas_call(
        functools.partial(_neighbourhood_kernel, n_rows=n_rows, group=group),
        out_shape=jax.ShapeDtypeStruct(qp.shape, BF16),
        grid=(b, kv_heads),
        in_specs=[
            pl.BlockSpec((1, s, w), lambda bi, g: (bi, 0, g)),
            pl.BlockSpec((1, s, HEAD_DIM), lambda bi, g: (bi, 0, g)),
            pl.BlockSpec((1, s, HEAD_DIM), lambda bi, g: (bi, 0, v_block0 + g)),
            pl.BlockSpec((NA_ROWS, group, GRID_W, NA_ROWS * GRID_W), lambda bi, g: (0, g, 0, 0)),
        ],
        out_specs=pl.BlockSpec((1, s, w), lambda bi, g: (bi, 0, g)),
        compiler_params=_params("parallel", "arbitrary"),
        name="neighbourhood_attention",
    )(qp, kp, proj, bias)


def _gain_rows(gain, n):
    return jnp.broadcast_to(gain.astype(F32).reshape(1, HEAD_DIM), (n, HEAD_DIM))


def mixer_axial_gqa(hn, x, w_in, w_out, q_gain, k_gain, b, s):
    proj = matmul(hn, w_in, out_dtype=BF16, tk=4096, name="a_in").reshape(b, s, -1)
    rope = axial_rope_tables(s)
    nq, nk = A_HEADS // 4, A_KV_HEADS // 4
    qp = prep_heads(proj, _gain_rows(q_gain, nq), lambda j: j, nq, rope=rope, scale=Q_SCALE, name="a_q")
    kp = prep_heads(proj, _gain_rows(k_gain, nk), lambda j: nq + j, nk, rope=rope, name="a_k")
    o = dense_gqa(qp, kp, proj, A_HEADS + A_KV_HEADS, kv_heads=A_KV_HEADS, group=A_HEADS // A_KV_HEADS)
    return matmul(o.reshape(b * s, -1), w_out, out_dtype=F32, res=x, tk=4096, name="a_out")


def mixer_dilated(hn, x, w_in, w_out, q_gain, k_gain, b, s):
    proj = matmul(hn, w_in, out_dtype=BF16, tk=4096, name="b_in").reshape(b, s, -1)
    rope = full_rope_tables(s)
    ng = len(B_CONFIGS)
    blocks = (B_HEADS + 2 * B_KV_HEADS) // 4
    q_blocks = B_HEADS // 4
    qg = jnp.repeat(q_gain.astype(F32), q_blocks, axis=0)
    qp = prep_heads(proj, qg, lambda j: blocks * (j // q_blocks) + j % q_blocks, ng * q_blocks,
                    rope=rope, scale=Q_SCALE, name="b_q")
    kp = prep_heads(proj, k_gain.astype(F32), lambda j: blocks * j + q_blocks, ng, rope=rope, name="b_k")
    o = dilated_attention(qp, kp, proj)
    return matmul(o.reshape(b * s, -1), w_out, out_dtype=F32, res=x, tk=2048, name="b_out")


def mixer_differential(hn, x, w_in, w_out, q_gain, k_gain, lam_params, subln, lambda_init, b, s):
    proj = matmul(hn, w_in, out_dtype=BF16, tk=4096, name="c_in").reshape(b, s, -1)
    rope = full_rope_tables(s)
    nb = 2 * C_HEADS // 4
    qp = prep_heads(proj, _gain_rows(q_gain, nb), lambda j: j, nb, rope=rope, scale=Q_SCALE, name="c_q")
    kp = prep_heads(proj, _gain_rows(k_gain, nb), lambda j: nb + j, nb, rope=rope, name="c_k")
    o = differential_attention(qp, kp, proj, lam_params, subln, lambda_init)
    return matmul(o.reshape(b * s, -1), w_out, out_dtype=F32, res=x, tk=4096, name="c_out")


def mixer_neighbourhood(hn, x, w_in, w_out, q_gain, k_gain, rpb, b, s):
    proj = matmul(hn, w_in, out_dtype=BF16, tk=4096, name="d_in").reshape(b, s, -1)
    nq, nk = D_HEADS // 4, D_KV_HEADS // 4
    qp = prep_heads(proj, _gain_rows(q_gain, nq), lambda j: j, nq, scale=Q_SCALE, name="d_q")
    kp = prep_heads(proj, _gain_rows(k_gain, nk), lambda j: nq + j, nk, name="d_k")
    bias = na_bias_table(rpb)
    o = neighbourhood_attention(qp, kp, proj, D_HEADS + D_KV_HEADS, bias,
                                kv_heads=D_KV_HEADS, group=D_HEADS // D_KV_HEADS)
    return matmul(o.reshape(b * s, -1), w_out, out_dtype=F32, res=x, tk=4096, name="d_out")


def squared_relu_mlp(x, gain, w_up, w_down):
    hn = rmsnorm(x, gain)
    u = matmul(hn, w_up, out_dtype=BF16, act="relu2", tk=4096, name="mlp_up")
    return matmul(u, w_down, out_dtype=F32, res=x, tk=2048, name="mlp_down")


def trunk(x, b, s, ln_mix, ln_ffn, w_up, w_down, mixers):
    depth = ln_mix.shape[0]
    for i in range(depth):
        m, j = i % N_MIXERS, i // N_MIXERS
        hn = rmsnorm(x, ln_mix[i])
        p = mixers[m]
        if m == 0:
            x = mixer_axial_gqa(hn, x, p["w_in"][j], p["w_out"][j], p["q_gain"][j], p["k_gain"][j], b, s)
        elif m == 1:
            x = mixer_dilated(hn, x, p["w_in"][j], p["w_out"][j], p["q_gain"][j], p["k_gain"][j], b, s)
        elif m == 2:
            lambda_init = 0.8 - 0.6 * math.exp(-0.3 * i)
            x = mixer_differential(hn, x, p["w_in"][j], p["w_out"][j], p["q_gain"][j], p["k_gain"][j],
                                   p["lam"][j], p["subln"][j], lambda_init, b, s)
        else:
            x = mixer_neighbourhood(hn, x, p["w_in"][j], p["w_out"][j], p["q_gain"][j], p["k_gain"][j],
                                    p["rpb"][j], b, s)
        x = squared_relu_mlp(x, ln_ffn[i], w_up[i], w_down[i])
    return x


def kernel(x_prompt, x_sample, ln_mix, ln_ffn, w_up, w_down, a_w_in, a_w_out, a_q_gain, a_k_gain, b_w_in, b_w_out, b_q_gain, b_k_gain, c_w_in, c_w_out, c_q_gain, c_k_gain, c_lambda, c_subln, d_w_in, d_w_out, d_q_gain, d_k_gain, d_rpb):
    bp, s, d = x_prompt.shape
    bs = x_sample.shape[0]
    assert x_sample.shape[1:] == (s, d)
    b = bp + bs
    x = jnp.concatenate([x_prompt, x_sample], axis=0).reshape(b * s, d)
    cast = lambda w: w.astype(BF16)
    mixers = [
        dict(w_in=cast(a_w_in), w_out=cast(a_w_out), q_gain=a_q_gain, k_gain=a_k_gain),
        dict(w_in=cast(b_w_in), w_out=cast(b_w_out), q_gain=b_q_gain, k_gain=b_k_gain),
        dict(w_in=cast(c_w_in), w_out=cast(c_w_out), q_gain=c_q_gain, k_gain=c_k_gain,
             lam=c_lambda, subln=c_subln),
        dict(w_in=cast(d_w_in), w_out=cast(d_w_out), q_gain=d_q_gain, k_gain=d_k_gain, rpb=d_rpb),
    ]
    y = trunk(x, b, s, ln_mix, ln_ffn, cast(w_up), cast(w_down), mixers).reshape(b, s, d)
    return (y[:bp], y[bp:])
```

```python
import functools
import math

import jax
import jax.numpy as jnp
from jax import lax
from jax.experimental import pallas as pl
from jax.experimental.pallas import tpu as pltpu

F32 = jnp.float32
BF16 = jnp.bfloat16

HEAD_DIM = 128
GRID_W = 64
ROPE_THETA = 10000.0
EPS = 1e-6
LOG2E = math.log2(math.e)
Q_SCALE = HEAD_DIM ** -0.5 * LOG2E
N_MIXERS = 4
A_HEADS, A_KV_HEADS = 32, 8
B_CONFIGS = ((128, 1), (512, 4), (2048, 16))
B_HEADS, B_KV_HEADS = 16, 4
C_HEADS = 16
D_HEADS, D_KV_HEADS = 32, 8
NA_ROWS, NA_COLS = 8, 16
MASKED = -1e30
RESIDUE_LIFT_BITS = 16

LANES = 128
V7X_VMEM_BYTES = 64 * 1024 * 1024
VMEM_LIMIT_BYTES = V7X_VMEM_BYTES - 8 * 1024 * 1024


def _tile(dim, pref, quantum):
    if dim <= pref:
        return dim
    t = (pref // quantum) * quantum
    while dim % t:
        t -= quantum
    return t


def _params(*semantics):
    return pltpu.CompilerParams(dimension_semantics=semantics, vmem_limit_bytes=VMEM_LIMIT_BYTES)


def _rmsnorm_kernel(x_ref, g_ref, o_ref):
    x = x_ref[...]
    ms = jnp.mean(x * x, axis=-1, keepdims=True)
    o_ref[...] = (x * lax.rsqrt(ms + EPS) * g_ref[...]).astype(o_ref.dtype)


def rmsnorm(x, gain):
    t, d = x.shape
    tr = _tile(t, 512, 8)
    return pl.pallas_call(
        _rmsnorm_kernel,
        out_shape=jax.ShapeDtypeStruct((t, d), BF16),
        grid=(t // tr,),
        in_specs=[pl.BlockSpec((tr, d), lambda i: (i, 0)), pl.BlockSpec((1, d), lambda i: (0, 0))],
        out_specs=pl.BlockSpec((tr, d), lambda i: (i, 0)),
        compiler_params=_params("parallel"),
        name="rmsnorm",
    )(x, gain.reshape(1, d).astype(F32))


def _matmul_kernel(a_ref, w_ref, *refs, nk, act, has_res, chunk):
    if has_res:
        res_ref, o_ref = refs
    else:
        res_ref = None
        (o_ref,) = refs
    k = pl.program_id(2)

    if nk > 1:
        @pl.when(k == 0)
        def _():
            o_ref[...] = res_ref[...] if has_res else jnp.zeros_like(o_ref)

    for c in range(o_ref.shape[1] // chunk):
        cols = slice(c * chunk, (c + 1) * chunk)
        part = jnp.dot(a_ref[...], w_ref[:, cols], preferred_element_type=F32)
        if nk > 1:
            o_ref[:, cols] += part
            continue
        if has_res:
            part = part + res_ref[:, cols]
        if act == "relu2":
            r = jnp.maximum(part, 0.0)
            part = r * r
        o_ref[:, cols] = part.astype(o_ref.dtype)


def matmul(a, w, *, out_dtype, act=None, res=None, rows=None, tm=1024, tn=1024, tk=2048, chunk=256,
           name="matmul"):
    row0, m = rows if rows is not None else (0, a.shape[0])
    kdim, n = w.shape
    tm, tn, tk = _tile(math.gcd(m, row0) if row0 else m, tm, 8), _tile(n, tn, LANES), _tile(kdim, tk, LANES)
    nk = kdim // tk
    t0 = row0 // tm
    chunk = min(chunk, tn)
    assert nk == 1 or (out_dtype == F32 and act is None), "a split K accumulates in the f32 output block"
    in_specs = [
        pl.BlockSpec((tm, tk), lambda i, j, k: (t0 + i, k)),
        pl.BlockSpec((tk, tn), lambda i, j, k: (k, j)),
    ]
    args = [a, w]
    if res is not None:
        in_specs.append(pl.BlockSpec((tm, tn), lambda i, j, k: (t0 + i, j)))
        args.append(res)
    return pl.pallas_call(
        functools.partial(_matmul_kernel, nk=nk, act=act, has_res=res is not None, chunk=chunk),
        out_shape=jax.ShapeDtypeStruct((m, n), out_dtype),
        grid=(m // tm, n // tn, nk),
        in_specs=in_specs,
        out_specs=pl.BlockSpec((tm, tn), lambda i, j, k: (i, j)),
        compiler_params=_params("parallel", "parallel", "arbitrary"),
        name=name,
    )(*args)


def rope_tables(pos, dim):
    inv = ROPE_THETA ** (-jnp.arange(0, dim, 2, dtype=F32) / dim)
    ang = pos.astype(F32)[:, None] * inv[None, :]
    ang = jnp.concatenate([ang, ang], axis=-1)
    return jnp.cos(ang), jnp.sin(ang)


def full_rope_tables(s):
    cos, sin = rope_tables(jnp.arange(s), HEAD_DIM)
    lane = jnp.arange(HEAD_DIM)
    sin_signed = jnp.where(lane < HEAD_DIM // 2, -sin, sin)
    return ((HEAD_DIM // 2,), (cos, sin_signed))


def axial_rope_tables(s):
    half = HEAD_DIM // 2
    t = jnp.arange(s)
    cr, sr = rope_tables(t // GRID_W, half)
    cc, sc = rope_tables(t % GRID_W, half)
    cos = jnp.concatenate([cr, cc], axis=-1)
    sin = jnp.concatenate([sr, sc], axis=-1)
    low = (jnp.arange(HEAD_DIM) % half) < half // 2
    return ((HEAD_DIM - half // 2, half // 2),
            (cos, jnp.where(low, -sin, 0.0), jnp.where(low, 0.0, sin)))


def _prep_kernel(x_ref, g_ref, *refs, shifts, scale, heads):
    *tab_refs, o_ref = refs
    g = g_ref[pl.ds(pl.program_id(2), 1), :]
    for h in range(heads):
        cols = slice(h * HEAD_DIM, (h + 1) * HEAD_DIM)
        x = x_ref[0, :, cols].astype(F32)
        ms = jnp.mean(x * x, axis=-1, keepdims=True)
        y = x * lax.rsqrt(ms + EPS) * g
        if shifts:
            r = y * tab_refs[0][...]
            for sh, t_ref in zip(shifts, tab_refs[1:]):
                r = r + pltpu.roll(y, sh, 1) * t_ref[...]
            y = r
        if scale != 1.0:
            y = y * scale
        o_ref[0, :, cols] = y.astype(o_ref.dtype)


def prep_heads(proj, gains, src_block, n_blocks, *, rope=None, scale=1.0, heads=4, name="prep"):
    b, s, _ = proj.shape
    w = heads * HEAD_DIM
    ts = _tile(s, 1024, 16)
    shifts, tables = rope if rope is not None else ((), ())
    in_specs = [
        pl.BlockSpec((1, ts, w), lambda bi, si, j: (bi, si, src_block(j))),
        pl.BlockSpec((n_blocks, HEAD_DIM), lambda bi, si, j: (0, 0)),
    ] + [pl.BlockSpec((ts, HEAD_DIM), lambda bi, si, j: (si, 0)) for _ in tables]
    return pl.pallas_call(
        functools.partial(_prep_kernel, shifts=shifts, scale=scale, heads=heads),
        out_shape=jax.ShapeDtypeStruct((b, s, n_blocks * w), BF16),
        grid=(b, s // ts, n_blocks),
        in_specs=in_specs,
        out_specs=pl.BlockSpec((1, ts, w), lambda bi, si, j: (bi, si, j)),
        compiler_params=_params("parallel", "parallel", "arbitrary"),
        name=name,
    )(proj, gains.astype(F32), *[t.astype(F32) for t in tables])


def _stack_heads(x, n):
    return jnp.concatenate([x[:, h * HEAD_DIM:(h + 1) * HEAD_DIM] for h in range(n)], axis=0)


def _unstack_heads(x, n):
    rows = x.shape[0] // n
    return jnp.concatenate([x[h * rows:(h + 1) * rows] for h in range(n)], axis=1)


def _qk(q, k):
    return lax.dot_general(q, k, (((1,), (1,)), ((), ())), preferred_element_type=F32)


def _dense_gqa_kernel(q_ref, k_ref, v_ref, o_ref, *, group, chains):
    k, v = k_ref[0], v_ref[0]
    per = group // chains

    def scores(c):
        return _qk(_stack_heads(q_ref[0, :, c * per * HEAD_DIM:(c + 1) * per * HEAD_DIM], per), k)

    nxt = scores(0)
    for c in range(chains):
        cols = slice(c * per * HEAD_DIM, (c + 1) * per * HEAD_DIM)
        s = nxt
        if c + 1 < chains:
            nxt = scores(c + 1)
        m = jnp.max(s, axis=-1, keepdims=True)
        p = jnp.exp2(s - m)
        l = jnp.sum(p, axis=-1, keepdims=True)
        o = jnp.dot(p.astype(BF16), v, preferred_element_type=F32) / l
        o_ref[0, :, cols] = _unstack_heads(o, per).astype(o_ref.dtype)


def dense_gqa(qp, kp, proj, v_block0, *, kv_heads, group, tq=256, chains=4):
    b, s, _ = qp.shape
    tq = _tile(s, tq, 16)
    w = group * HEAD_DIM
    return pl.pallas_call(
        functools.partial(_dense_gqa_kernel, group=group, chains=chains),
        out_shape=jax.ShapeDtypeStruct(qp.shape, BF16),
        grid=(b, kv_heads, s // tq),
        in_specs=[
            pl.BlockSpec((1, tq, w), lambda bi, g, i: (bi, i, g)),
            pl.BlockSpec((1, s, HEAD_DIM), lambda bi, g, i: (bi, 0, g)),
            pl.BlockSpec((1, s, HEAD_DIM), lambda bi, g, i: (bi, 0, v_block0 + g)),
        ],
        out_specs=pl.BlockSpec((1, tq, w), lambda bi, g, i: (bi, i, g)),
        compiler_params=_params("parallel", "parallel", "arbitrary"),
        name="dense_gqa",
    )(qp, kp, proj)


def _dilated_kernel(*refs, configs, seq, group):
    o_ref = refs[-1]
    i = pl.program_id(2)
    tq = o_ref.shape[1]

    chains = []
    for gi, (window, dil) in enumerate(configs):
        reach = (window // (2 * dil)) * dil
        pad = -(-reach // tq) * tq
        wlen = min(seq, tq + 2 * pad)
        start = pl.multiple_of(jnp.clip(i * tq - pad, 0, seq - wlen), tq)
        u = (start + lax.broadcasted_iota(jnp.int32, (tq, wlen), 1)
             - i * tq - lax.broadcasted_iota(jnp.int32, (tq, wlen), 0) + reach)
        if dil > 1:
            u = u + ((u & (dil - 1)) << RESIDUE_LIFT_BITS)
        ok = lax.bitcast_convert_type(u, jnp.uint32) <= jnp.uint32(2 * reach)
        bias = jnp.where(ok, 0.0, MASKED)
        per = group if wlen <= 4 * tq else max(1, group // 2)
        for h0 in range(0, group, per):
            chains.append((gi, h0, per, start, wlen, bias))

    def scores(c):
        gi, h0, per, start, wlen, _ = chains[c]
        q_ref, k_ref = refs[3 * gi], refs[3 * gi + 1]
        q = _stack_heads(q_ref[0, :, h0 * HEAD_DIM:(h0 + per) * HEAD_DIM], per)
        return _qk(q, k_ref[0, pl.ds(start, wlen), :])

    outs = [[None] * group for _ in configs]
    lses = [[None] * group for _ in configs]
    nxt = scores(0)
    for c, (gi, h0, per, start, wlen, bias) in enumerate(chains):
        s = nxt
        if c + 1 < len(chains):
            nxt = scores(c + 1)
        s = (s.reshape(per, tq, wlen) + bias[None]).reshape(per * tq, wlen)
        m = jnp.max(s, axis=-1, keepdims=True)
        p = jnp.exp2(s - m)
        l = jnp.sum(p, axis=-1, keepdims=True)
        v_ref = refs[3 * gi + 2]
        o = jnp.dot(p.astype(BF16), v_ref[0, pl.ds(start, wlen), :], preferred_element_type=F32) / l
        lse = m + jnp.log2(l)
        for h in range(per):
            outs[gi][h0 + h] = o[h * tq:(h + 1) * tq]
            lses[gi][h0 + h] = lse[h * tq:(h + 1) * tq]

    for h in range(group):
        lse_h = [lses[gi][h] for gi in range(len(configs))]
        top = functools.reduce(jnp.maximum, lse_h)
        ws = [jnp.exp2(x - top) for x in lse_h]
        o = sum(w * outs[gi][h] for gi, w in enumerate(ws)) / sum(ws)
        o_ref[0, :, h * HEAD_DIM:(h + 1) * HEAD_DIM] = o.astype(o_ref.dtype)


def dilated_attention(qp, kp, proj, *, configs=B_CONFIGS, tq=256):
    b, s, _ = proj.shape
    tq = _tile(s, tq, LANES)
    for window, dil in configs:
        assert dil & (dil - 1) == 0, "the residue test uses a power-of-two dilation"
        assert 2 * window + s < 1 << RESIDUE_LIFT_BITS, "a lifted residue must clear every in-range offset"
    group = B_HEADS // B_KV_HEADS
    w = group * HEAD_DIM
    per_group_blocks = B_HEADS + 2 * B_KV_HEADS
    in_specs, args = [], []
    for gi in range(len(configs)):
        in_specs += [
            pl.BlockSpec((1, tq, w), lambda bi, h, i, gi=gi: (bi, i, gi * B_KV_HEADS + h)),
            pl.BlockSpec((1, s, HEAD_DIM), lambda bi, h, i, gi=gi: (bi, 0, gi * B_KV_HEADS + h)),
            pl.BlockSpec((1, s, HEAD_DIM),
                         lambda bi, h, i, gi=gi: (bi, 0, gi * per_group_blocks + B_HEADS + B_KV_HEADS + h)),
        ]
        args += [qp, kp, proj]
    return pl.pallas_call(
        functools.partial(_dilated_kernel, configs=configs, seq=s, group=group),
        out_shape=jax.ShapeDtypeStruct((b, s, B_HEADS * HEAD_DIM), BF16),
        grid=(b, B_KV_HEADS, s // tq),
        in_specs=in_specs,
        out_specs=pl.BlockSpec((1, tq, w), lambda bi, h, i: (bi, i, h)),
        compiler_params=_params("parallel", "parallel", "arbitrary"),
        name="dilated_attention",
    )(*args)


def _differential_kernel(q_ref, k_ref, v_ref, lam_ref, sub_ref, o_ref, *, lambda_init, chains):
    lp = lam_ref[...]
    lam = (jnp.exp(jnp.sum(lp[0:1] * lp[1:2], axis=-1, keepdims=True))
           - jnp.exp(jnp.sum(lp[2:3] * lp[3:4], axis=-1, keepdims=True)) + lambda_init)
    k, v = k_ref[0], v_ref[0]
    rows = q_ref.shape[1] // chains
    def scores(c):
        q = q_ref[0, c * rows:(c + 1) * rows, :]
        return [_qk(q[:, mi * HEAD_DIM:(mi + 1) * HEAD_DIM], k[:, mi * HEAD_DIM:(mi + 1) * HEAD_DIM])
                for mi in range(2)]

    nxt = scores(0)
    for c in range(chains):
        cur = nxt
        if c + 1 < chains:
            nxt = scores(c + 1)
        probs = []
        for s in cur:
            m = jnp.max(s, axis=-1, keepdims=True)
            p = jnp.exp2(s - m)
            probs.append((p, jnp.sum(p, axis=-1, keepdims=True)))
        (p0, l0), (p1, l1) = probs
        a = p0 - p1 * (lam * l0 / l1)
        o = jnp.dot(a.astype(BF16), v, preferred_element_type=F32) / l0
        ms = jnp.mean(o * o, axis=-1, keepdims=True)
        o = o * lax.rsqrt(ms + EPS) * sub_ref[...] * (1.0 - lambda_init)
        o_ref[0, c * rows:(c + 1) * rows, :] = o.astype(o_ref.dtype)


def differential_attention(qp, kp, proj, lam_params, subln, lambda_init, *, tq=1024, chains=4):
    b, s, _ = qp.shape
    tq = _tile(s, tq, 16)
    w = 2 * HEAD_DIM
    return pl.pallas_call(
        functools.partial(_differential_kernel, lambda_init=lambda_init, chains=chains),
        out_shape=jax.ShapeDtypeStruct(qp.shape, BF16),
        grid=(b, C_HEADS, s // tq),
        in_specs=[
            pl.BlockSpec((1, tq, w), lambda bi, h, i: (bi, i, h)),
            pl.BlockSpec((1, s, w), lambda bi, h, i: (bi, 0, h)),
            pl.BlockSpec((1, s, w), lambda bi, h, i: (bi, 0, 2 * C_HEADS + h)),
            pl.BlockSpec((4, HEAD_DIM), lambda bi, h, i: (0, 0)),
            pl.BlockSpec((1, w), lambda bi, h, i: (0, 0)),
        ],
        out_specs=pl.BlockSpec((1, tq, w), lambda bi, h, i: (bi, i, h)),
        compiler_params=_params("parallel", "parallel", "arbitrary"),
        name="differential_attention",
    )(qp, kp, proj, lam_params.astype(F32), subln.reshape(1, w).astype(F32))


def _na_bias_kernel(rpb_ref, o_ref):
    rows = lax.broadcasted_iota(jnp.int32, (GRID_W, LANES), 0)
    lane = lax.broadcasted_iota(jnp.int32, (GRID_W, LANES), 1)
    n_rel = 2 * NA_ROWS - 1
    toeplitz = []
    for rho in range(n_rel):
        z = jnp.broadcast_to(rpb_ref[0, rho:rho + 1, :], (GRID_W, LANES))
        z = pltpu.roll(z, LANES - (NA_COLS - 1), 1)
        bit = 1
        while bit < GRID_W:
            z = jnp.where((rows & bit) != 0, pltpu.roll(z, bit, 1), z)
            bit *= 2
        toeplitz.append(z)
    kc = lane & (GRID_W - 1)
    first = jnp.clip(rows - NA_COLS // 2, 0, GRID_W - NA_COLS)
    inside = (kc >= first) & (kc < first + NA_COLS)
    for d in range(NA_ROWS):
        for pair in range(NA_ROWS // 2):
            even = toeplitz[d + 2 * pair]
            odd = pltpu.roll(toeplitz[d + 2 * pair + 1], GRID_W, 1)
            tile = jnp.where(lane < GRID_W, even, odd)
            o_ref[d, 0, :, pair * LANES:(pair + 1) * LANES] = jnp.where(inside, tile * LOG2E, MASKED)


def na_bias_table(rpb):
    h = rpb.shape[0]
    padded = jnp.zeros((h, 2 * NA_ROWS, LANES), F32).at[:, :2 * NA_ROWS - 1, :2 * NA_COLS - 1].set(rpb.astype(F32))
    return pl.pallas_call(
        _na_bias_kernel,
        out_shape=jax.ShapeDtypeStruct((NA_ROWS, h, GRID_W, NA_ROWS * GRID_W), F32),
        grid=(h,),
        in_specs=[pl.BlockSpec((1, 2 * NA_ROWS, LANES), lambda i: (i, 0, 0))],
        out_specs=pl.BlockSpec((NA_ROWS, 1, GRID_W, NA_ROWS * GRID_W), lambda i: (0, i, 0, 0)),
        compiler_params=_params("parallel"),
        name="na_bias_table",
    )(padded)


def _neighbourhood_kernel(q_ref, k_ref, v_ref, b_ref, o_ref, *, n_rows, group):
    win = NA_ROWS * GRID_W

    def row(r, carry):
        first = jnp.clip(r - NA_ROWS // 2, 0, n_rows - NA_ROWS)
        q0 = pl.multiple_of(r * GRID_W, GRID_W)
        k0 = pl.multiple_of(first * GRID_W, GRID_W)
        q = _stack_heads(q_ref[0, pl.ds(q0, GRID_W), :], group)
        s = _qk(q, k_ref[0, pl.ds(k0, win), :])
        s = (s.reshape(group, GRID_W, win) + b_ref[first - r + NA_ROWS - 1]).reshape(group * GRID_W, win)
        m = jnp.max(s, axis=-1, keepdims=True)
        p = jnp.exp2(s - m)
        l = jnp.sum(p, axis=-1, keepdims=True)
        o = jnp.dot(p.astype(BF16), v_ref[0, pl.ds(k0, win), :], preferred_element_type=F32) / l
        o_ref[0, pl.ds(q0, GRID_W), :] = _unstack_heads(o, group).astype(o_ref.dtype)
        return carry

    lax.fori_loop(0, n_rows, row, 0, unroll=4)


def neighbourhood_attention(qp, kp, proj, v_block0, bias, *, kv_heads, group):
    b, s, _ = qp.shape
    n_rows = s // GRID_W
    assert n_rows >= NA_ROWS, "the key window is NA_ROWS grid rows tall"
    w = group * HEAD_DIM
    return pl.pallas_call(
        functools.partial(_neighbourhood_kernel, n_rows=n_rows, group=group),
        out_shape=jax.ShapeDtypeStruct(qp.shape, BF16),
        grid=(b, kv_heads),
        in_specs=[
            pl.BlockSpec((1, s, w), lambda bi, g: (bi, 0, g)),
            pl.BlockSpec((1, s, HEAD_DIM), lambda bi, g: (bi, 0, g)),
            pl.BlockSpec((1, s, HEAD_DIM), lambda bi, g: (bi, 0, v_block0 + g)),
            pl.BlockSpec((NA_ROWS, group, GRID_W, NA_ROWS * GRID_W), lambda bi, g: (0, g, 0, 0)),
        ],
        out_specs=pl.BlockSpec((1, s, w), lambda bi, g: (bi, 0, g)),
        compiler_params=_params("parallel", "arbitrary"),
        name="neighbourhood_attention",
    )(qp, kp, proj, bias)


def _gain_rows(gain, n):
    return jnp.broadcast_to(gain.astype(F32).reshape(1, HEAD_DIM), (n, HEAD_DIM))


def mixer_axial_gqa(hn, x, w_in, w_out, q_gain, k_gain, b, s):
    proj = matmul(hn, w_in, out_dtype=BF16, tk=4096, name="a_in").reshape(b, s, -1)
    rope = axial_rope_tables(s)
    nq, nk = A_HEADS // 4, A_KV_HEADS // 4
    qp = prep_heads(proj, _gain_rows(q_gain, nq), lambda j: j, nq, rope=rope, scale=Q_SCALE, name="a_q")
    kp = prep_heads(proj, _gain_rows(k_gain, nk), lambda j: nq + j, nk, rope=rope, name="a_k")
    o = dense_gqa(qp, kp, proj, A_HEADS + A_KV_HEADS, kv_heads=A_KV_HEADS, group=A_HEADS // A_KV_HEADS)
    return matmul(o.reshape(b * s, -1), w_out, out_dtype=F32, res=x, tk=4096, name="a_out")


def mixer_dilated(hn, x, w_in, w_out, q_gain, k_gain, b, s):
    proj = matmul(hn, w_in, out_dtype=BF16, tk=4096, name="b_in").reshape(b, s, -1)
    rope = full_rope_tables(s)
    ng = len(B_CONFIGS)
    blocks = (B_HEADS + 2 * B_KV_HEADS) // 4
    q_blocks = B_HEADS // 4
    qg = jnp.repeat(q_gain.astype(F32), q_blocks, axis=0)
    qp = prep_heads(proj, qg, lambda j: blocks * (j // q_blocks) + j % q_blocks, ng * q_blocks,
                    rope=rope, scale=Q_SCALE, name="b_q")
    kp = prep_heads(proj, k_gain.astype(F32), lambda j: blocks * j + q_blocks, ng, rope=rope, name="b_k")
    o = dilated_attention(qp, kp, proj)
    return matmul(o.reshape(b * s, -1), w_out, out_dtype=F32, res=x, tk=2048, name="b_out")


def mixer_differential(hn, x, w_in, w_out, q_gain, k_gain, lam_params, subln, lambda_init, b, s):
    proj = matmul(hn, w_in, out_dtype=BF16, tk=4096, name="c_in").reshape(b, s, -1)
    rope = full_rope_tables(s)
    nb = 2 * C_HEADS // 4
    qp = prep_heads(proj, _gain_rows(q_gain, nb), lambda j: j, nb, rope=rope, scale=Q_SCALE, name="c_q")
    kp = prep_heads(proj, _gain_rows(k_gain, nb), lambda j: nb + j, nb, rope=rope, name="c_k")
    o = differential_attention(qp, kp, proj, lam_params, subln, lambda_init)
    return matmul(o.reshape(b * s, -1), w_out, out_dtype=F32, res=x, tk=4096, name="c_out")


def mixer_neighbourhood(hn, x, w_in, w_out, q_gain, k_gain, rpb, b, s):
    proj = matmul(hn, w_in, out_dtype=BF16, tk=4096, name="d_in").reshape(b, s, -1)
    nq, nk = D_HEADS // 4, D_KV_HEADS // 4
    qp = prep_heads(proj, _gain_rows(q_gain, nq), lambda j: j, nq, scale=Q_SCALE, name="d_q")
    kp = prep_heads(proj, _gain_rows(k_gain, nk), lambda j: nq + j, nk, name="d_k")
    bias = na_bias_table(rpb)
    o = neighbourhood_attention(qp, kp, proj, D_HEADS + D_KV_HEADS, bias,
                                kv_heads=D_KV_HEADS, group=D_HEADS // D_KV_HEADS)
    return matmul(o.reshape(b * s, -1), w_out, out_dtype=F32, res=x, tk=4096, name="d_out")


def squared_relu_mlp(x, gain, w_up, w_down, split=None):
    hn = rmsnorm(x, gain)
    u = matmul(hn, w_up, out_dtype=BF16, act="relu2", tk=4096, name="mlp_up")
    down = functools.partial(matmul, u, w_down, out_dtype=F32, res=x, tk=2048, name="mlp_down")
    if split is None:
        return down()
    return down(rows=(0, split)), down(rows=(split, x.shape[0] - split))


def trunk(x, b, s, ln_mix, ln_ffn, w_up, w_down, mixers, split):
    depth = ln_mix.shape[0]
    for i in range(depth):
        m, j = i % N_MIXERS, i // N_MIXERS
        hn = rmsnorm(x, ln_mix[i])
        p = mixers[m]
        if m == 0:
            x = mixer_axial_gqa(hn, x, p["w_in"][j], p["w_out"][j], p["q_gain"][j], p["k_gain"][j], b, s)
        elif m == 1:
            x = mixer_dilated(hn, x, p["w_in"][j], p["w_out"][j], p["q_gain"][j], p["k_gain"][j], b, s)
        elif m == 2:
            lambda_init = 0.8 - 0.6 * math.exp(-0.3 * i)
            x = mixer_differential(hn, x, p["w_in"][j], p["w_out"][j], p["q_gain"][j], p["k_gain"][j],
                                   p["lam"][j], p["subln"][j], lambda_init, b, s)
        else:
            x = mixer_neighbourhood(hn, x, p["w_in"][j], p["w_out"][j], p["q_gain"][j], p["k_gain"][j],
                                    p["rpb"][j], b, s)
        x = squared_relu_mlp(x, ln_ffn[i], w_up[i], w_down[i], split=split if i == depth - 1 else None)
    return x


def kernel(x_prompt, x_sample, ln_mix, ln_ffn, w_up, w_down, a_w_in, a_w_out, a_q_gain, a_k_gain, b_w_in, b_w_out, b_q_gain, b_k_gain, c_w_in, c_w_out, c_q_gain, c_k_gain, c_lambda, c_subln, d_w_in, d_w_out, d_q_gain, d_k_gain, d_rpb):
    bp, s, d = x_prompt.shape
    bs = x_sample.shape[0]
    assert x_sample.shape[1:] == (s, d)
    b = bp + bs
    x = jnp.concatenate([x_prompt, x_sample], axis=0).reshape(b * s, d)
    cast = lambda w: w.astype(BF16)
    mixers = [
        dict(w_in=cast(a_w_in), w_out=cast(a_w_out), q_gain=a_q_gain, k_gain=a_k_gain),
        dict(w_in=cast(b_w_in), w_out=cast(b_w_out), q_gain=b_q_gain, k_gain=b_k_gain),
        dict(w_in=cast(c_w_in), w_out=cast(c_w_out), q_gain=c_q_gain, k_gain=c_k_gain,
             lam=c_lambda, subln=c_subln),
        dict(w_in=cast(d_w_in), w_out=cast(d_w_out), q_gain=d_q_gain, k_gain=d_k_gain, rpb=d_rpb),
    ]
    y_prompt, y_sample = trunk(x, b, s, ln_mix, ln_ffn, cast(w_up), cast(w_down), mixers, split=bp * s)
    return (y_prompt.reshape(bp, s, d), y_sample.reshape(bs, s, d))
```

```python
import functools
import math

import jax
import jax.numpy as jnp
from jax import lax
from jax.experimental import pallas as pl
from jax.experimental.pallas import tpu as pltpu

F32 = jnp.float32
BF16 = jnp.bfloat16

HEAD_DIM = 128
GRID_W = 64
ROPE_THETA = 10000.0
EPS = 1e-6
LOG2E = math.log2(math.e)
Q_SCALE = HEAD_DIM ** -0.5 * LOG2E
N_MIXERS = 4
A_HEADS, A_KV_HEADS = 32, 8
B_CONFIGS = ((128, 1), (512, 4), (2048, 16))
B_HEADS, B_KV_HEADS = 16, 4
C_HEADS = 16
D_HEADS, D_KV_HEADS = 32, 8
NA_ROWS, NA_COLS = 8, 16
MASKED = -1e30
RESIDUE_LIFT_BITS = 16

LANES = 128
V7X_VMEM_BYTES = 64 * 1024 * 1024
VMEM_LIMIT_BYTES = V7X_VMEM_BYTES - 8 * 1024 * 1024


def _tile(dim, pref, quantum):
    if dim <= pref:
        return dim
    t = (pref // quantum) * quantum
    while dim % t:
        t -= quantum
    return t


def _params(*semantics):
    return pltpu.CompilerParams(dimension_semantics=semantics, vmem_limit_bytes=VMEM_LIMIT_BYTES)


def _row_parts(x):
    return x if isinstance(x, tuple) else (x,)


def _row_part_specs(parts, tile_rows, width, row_tile, col_tile):
    specs, first = [], 0
    for p in parts:
        n = p.shape[0] // tile_rows
        specs.append(pl.BlockSpec(
            (tile_rows, width),
            lambda *g, first=first, n=n: (jnp.clip(row_tile(*g) - first, 0, n - 1), col_tile(*g))))
        first += n
    return specs


def _read_part(refs, part_tiles, row_tile, read):
    value, first = None, 0
    for ref, n in zip(refs, part_tiles):
        value = read(ref) if value is None else jnp.where(row_tile >= first, read(ref), value)
        first += n
    return value


def _rmsnorm_kernel(*refs, part_tiles):
    *x_refs, g_ref, o_ref = refs

    def emit(x_ref):
        x = x_ref[...]
        ms = jnp.mean(x * x, axis=-1, keepdims=True)
        o_ref[...] = (x * lax.rsqrt(ms + EPS) * g_ref[...]).astype(o_ref.dtype)

    first = 0
    for x_ref, n in zip(x_refs, part_tiles):
        i = pl.program_id(0)
        pl.when((i >= first) & (i < first + n))(functools.partial(emit, x_ref))
        first += n


def rmsnorm(x, gain):
    parts = _row_parts(x)
    d = parts[0].shape[1]
    t = sum(p.shape[0] for p in parts)
    tr = _tile(math.gcd(*[p.shape[0] for p in parts]), 512, 8)
    return pl.pallas_call(
        functools.partial(_rmsnorm_kernel, part_tiles=[p.shape[0] // tr for p in parts]),
        out_shape=jax.ShapeDtypeStruct((t, d), BF16),
        grid=(t // tr,),
        in_specs=_row_part_specs(parts, tr, d, lambda i: i, lambda i: 0) + [pl.BlockSpec((1, d), lambda i: (0, 0))],
        out_specs=pl.BlockSpec((tr, d), lambda i: (i, 0)),
        compiler_params=_params("parallel"),
        name="rmsnorm",
    )(*parts, gain.reshape(1, d).astype(F32))


def _matmul_kernel(a_ref, w_ref, *refs, nk, act, res_tiles, first_tile, chunk):
    *res_refs, o_ref = refs
    k = pl.program_id(2)

    def residual(cols):
        return _read_part(res_refs, res_tiles, first_tile + pl.program_id(0), lambda ref: ref[:, cols])

    if nk > 1:
        @pl.when(k == 0)
        def _():
            o_ref[...] = residual(slice(None)) if res_refs else jnp.zeros_like(o_ref)

    for c in range(o_ref.shape[1] // chunk):
        cols = slice(c * chunk, (c + 1) * chunk)
        part = jnp.dot(a_ref[...], w_ref[:, cols], preferred_element_type=F32)
        if nk > 1:
            o_ref[:, cols] += part
            continue
        if res_refs:
            part = part + residual(cols)
        if act == "relu2":
            r = jnp.maximum(part, 0.0)
            part = r * r
        o_ref[:, cols] = part.astype(o_ref.dtype)


def matmul(a, w, *, out_dtype, layer=None, act=None, res=None, rows=None,
           tm=1024, tn=1024, tk=2048, chunk=256, name="matmul"):
    row0, m = rows if rows is not None else (0, a.shape[0])
    kdim, n = w.shape[-2:]
    tm, tn, tk = _tile(math.gcd(m, row0) if row0 else m, tm, 8), _tile(n, tn, LANES), _tile(kdim, tk, LANES)
    nk = kdim // tk
    t0 = row0 // tm
    chunk = min(chunk, tn)
    assert nk == 1 or (out_dtype == F32 and act is None), "a split K accumulates in the f32 output block"
    if w.ndim == 3:
        w_spec = pl.BlockSpec((None, tk, tn), lambda i, j, k: (layer, k, j))
    else:
        w_spec = pl.BlockSpec((tk, tn), lambda i, j, k: (k, j))
    in_specs = [pl.BlockSpec((tm, tk), lambda i, j, k: (t0 + i, k)), w_spec]
    args = [a, w]
    res_parts = _row_parts(res) if res is not None else ()
    assert all(p.shape[0] % tm == 0 for p in res_parts)
    in_specs += _row_part_specs(res_parts, tm, tn, lambda i, j, k: t0 + i, lambda i, j, k: j)
    args += res_parts
    return pl.pallas_call(
        functools.partial(_matmul_kernel, nk=nk, act=act, chunk=chunk, first_tile=t0,
                          res_tiles=[p.shape[0] // tm for p in res_parts]),
        out_shape=jax.ShapeDtypeStruct((m, n), out_dtype),
        grid=(m // tm, n // tn, nk),
        in_specs=in_specs,
        out_specs=pl.BlockSpec((tm, tn), lambda i, j, k: (i, j)),
        compiler_params=_params("parallel", "parallel", "arbitrary"),
        name=name,
    )(*args)


def rope_tables(pos, dim):
    inv = ROPE_THETA ** (-jnp.arange(0, dim, 2, dtype=F32) / dim)
    ang = pos.astype(F32)[:, None] * inv[None, :]
    ang = jnp.concatenate([ang, ang], axis=-1)
    return jnp.cos(ang), jnp.sin(ang)


def full_rope_tables(s):
    cos, sin = rope_tables(jnp.arange(s), HEAD_DIM)
    lane = jnp.arange(HEAD_DIM)
    sin_signed = jnp.where(lane < HEAD_DIM // 2, -sin, sin)
    return ((HEAD_DIM // 2,), (cos, sin_signed))


def axial_rope_tables(s):
    half = HEAD_DIM // 2
    t = jnp.arange(s)
    cr, sr = rope_tables(t // GRID_W, half)
    cc, sc = rope_tables(t % GRID_W, half)
    cos = jnp.concatenate([cr, cc], axis=-1)
    sin = jnp.concatenate([sr, sc], axis=-1)
    low = (jnp.arange(HEAD_DIM) % half) < half // 2
    return ((HEAD_DIM - half // 2, half // 2),
            (cos, jnp.where(low, -sin, 0.0), jnp.where(low, 0.0, sin)))


def _prep_kernel(x_ref, gain_ref, *refs, shifts):
    *tab_refs, o_ref = refs
    tables = [t[...] for t in tab_refs]
    averager = jnp.full((HEAD_DIM, HEAD_DIM), 1.0 / HEAD_DIM, BF16)
    for h0 in range(0, x_ref.shape[1], HEAD_DIM):
        head = slice(h0, h0 + HEAD_DIM)
        x = x_ref[:, head].astype(F32)
        ms = jnp.dot((x * x).astype(BF16), averager, preferred_element_type=F32)
        y = x * lax.rsqrt(ms + EPS) * gain_ref[:, head]
        if shifts:
            r = y * tables[0]
            for sh, t in zip(shifts, tables[1:]):
                r = r + pltpu.roll(y, sh, 1) * t
            y = r
        o_ref[:, head] = y.astype(o_ref.dtype)


def in_projection(hn, w_in, layer, segments, rope, seq, tiles_per_group, v_tile, *, tn, name):
    gains = jnp.concatenate([jnp.tile(jnp.ones((HEAD_DIM,), F32) if g is None else g.astype(F32), n)
                             for g, n in segments])[None]
    shifts, tables = rope if rope is not None else ((), ())
    proj = matmul(hn, w_in, layer=layer, out_dtype=BF16, tk=hn.shape[1], name=name)
    t, n = proj.shape
    ts = _tile(seq, 1024, 16)
    per_seq = seq // ts
    qk_tiles = n // tn // tiles_per_group * v_tile
    tile = lambda j: j // v_tile * tiles_per_group + j % v_tile
    return pl.pallas_call(
        functools.partial(_prep_kernel, shifts=shifts),
        out_shape=jax.ShapeDtypeStruct((t, n), BF16),
        grid=(t // ts, qk_tiles),
        in_specs=[pl.BlockSpec((ts, tn), lambda i, j: (i, tile(j))), pl.BlockSpec((1, tn), lambda i, j: (0, tile(j)))]
        + [pl.BlockSpec((ts, HEAD_DIM), lambda i, j: (i % per_seq, 0))] * len(tables),
        out_specs=pl.BlockSpec((ts, tn), lambda i, j: (i, tile(j))),
        input_output_aliases={0: 0},
        compiler_params=_params("parallel", "parallel"),
        name=name + "_prep",
    )(proj, gains, *[tab.astype(F32) for tab in tables])


def _stack_heads(x, n):
    return jnp.concatenate([x[:, h * HEAD_DIM:(h + 1) * HEAD_DIM] for h in range(n)], axis=0)


def _unstack_heads(x, n):
    rows = x.shape[0] // n
    return jnp.concatenate([x[h * rows:(h + 1) * rows] for h in range(n)], axis=1)


def _qk(q, k):
    return lax.dot_general(q, k, (((1,), (1,)), ((), ())), preferred_element_type=F32)


def _dense_gqa_kernel(q_ref, k_ref, v_ref, o_ref, *, group, chains):
    k, v = k_ref[0], v_ref[0]
    per = group // chains

    def scores(c):
        return _qk(_stack_heads(q_ref[0, :, c * per * HEAD_DIM:(c + 1) * per * HEAD_DIM], per), k)

    nxt = scores(0)
    for c in range(chains):
        cols = slice(c * per * HEAD_DIM, (c + 1) * per * HEAD_DIM)
        s = nxt
        if c + 1 < chains:
            nxt = scores(c + 1)
        m = jnp.max(s, axis=-1, keepdims=True)
        p = jnp.exp2(s - m)
        l = jnp.sum(p, axis=-1, keepdims=True)
        o = jnp.dot(p.astype(BF16), v, preferred_element_type=F32) / l
        o_ref[0, :, cols] = _unstack_heads(o, per).astype(o_ref.dtype)


def dense_gqa(proj, *, kv_heads, group, tq=256, chains=4):
    b, s, _ = proj.shape
    tq = _tile(s, tq, 16)
    w = group * HEAD_DIM
    k_block0 = kv_heads * group
    v_block0 = k_block0 + kv_heads
    return pl.pallas_call(
        functools.partial(_dense_gqa_kernel, group=group, chains=chains),
        out_shape=jax.ShapeDtypeStruct((b, s, kv_heads * w), BF16),
        grid=(b, kv_heads, s // tq),
        in_specs=[
            pl.BlockSpec((1, tq, w), lambda bi, g, i: (bi, i, g)),
            pl.BlockSpec((1, s, HEAD_DIM), lambda bi, g, i: (bi, 0, k_block0 + g)),
            pl.BlockSpec((1, s, HEAD_DIM), lambda bi, g, i: (bi, 0, v_block0 + g)),
        ],
        out_specs=pl.BlockSpec((1, tq, w), lambda bi, g, i: (bi, i, g)),
        compiler_params=_params("parallel", "parallel", "arbitrary"),
        name="dense_gqa",
    )(proj, proj, proj)


def _dilated_kernel(*refs, configs, seq, group):
    o_ref = refs[-1]
    i = pl.program_id(2)
    tq = o_ref.shape[1]

    chains = []
    for gi, (window, dil) in enumerate(configs):
        reach = (window // (2 * dil)) * dil
        pad = -(-reach // tq) * tq
        wlen = min(seq, tq + 2 * pad)
        start = pl.multiple_of(jnp.clip(i * tq - pad, 0, seq - wlen), tq)
        u = (start + lax.broadcasted_iota(jnp.int32, (tq, wlen), 1)
             - i * tq - lax.broadcasted_iota(jnp.int32, (tq, wlen), 0) + reach)
        if dil > 1:
            u = u + ((u & (dil - 1)) << RESIDUE_LIFT_BITS)
        ok = lax.bitcast_convert_type(u, jnp.uint32) <= jnp.uint32(2 * reach)
        bias = jnp.where(ok, 0.0, MASKED)
        per = group if wlen <= 4 * tq else max(1, group // 2)
        for h0 in range(0, group, per):
            chains.append((gi, h0, per, start, wlen, bias))

    def scores(c):
        gi, h0, per, start, wlen, _ = chains[c]
        q_ref, k_ref = refs[3 * gi], refs[3 * gi + 1]
        q = _stack_heads(q_ref[0, :, h0 * HEAD_DIM:(h0 + per) * HEAD_DIM], per)
        return _qk(q, k_ref[0, pl.ds(start, wlen), :])

    outs = [[None] * group for _ in configs]
    lses = [[None] * group for _ in configs]
    nxt = scores(0)
    for c, (gi, h0, per, start, wlen, bias) in enumerate(chains):
        s = nxt
        if c + 1 < len(chains):
            nxt = scores(c + 1)
        s = (s.reshape(per, tq, wlen) + bias[None]).reshape(per * tq, wlen)
        m = jnp.max(s, axis=-1, keepdims=True)
        p = jnp.exp2(s - m)
        l = jnp.sum(p, axis=-1, keepdims=True)
        v_ref = refs[3 * gi + 2]
        o = jnp.dot(p.astype(BF16), v_ref[0, pl.ds(start, wlen), :], preferred_element_type=F32) / l
        lse = m + jnp.log2(l)
        for h in range(per):
            outs[gi][h0 + h] = o[h * tq:(h + 1) * tq]
            lses[gi][h0 + h] = lse[h * tq:(h + 1) * tq]

    for h in range(group):
        lse_h = [lses[gi][h] for gi in range(len(configs))]
        top = functools.reduce(jnp.maximum, lse_h)
        ws = [jnp.exp2(x - top) for x in lse_h]
        o = sum(w * outs[gi][h] for gi, w in enumerate(ws)) / sum(ws)
        o_ref[0, :, h * HEAD_DIM:(h + 1) * HEAD_DIM] = o.astype(o_ref.dtype)


def dilated_attention(proj, *, configs=B_CONFIGS, tq=256):
    b, s, _ = proj.shape
    tq = _tile(s, tq, LANES)
    for window, dil in configs:
        assert dil & (dil - 1) == 0, "the residue test uses a power-of-two dilation"
        assert 2 * window + s < 1 << RESIDUE_LIFT_BITS, "a lifted residue must clear every in-range offset"
    group = B_HEADS // B_KV_HEADS
    w = group * HEAD_DIM
    per_group_blocks = B_HEADS + 2 * B_KV_HEADS
    in_specs, args = [], []
    for gi in range(len(configs)):
        q0 = gi * per_group_blocks // group
        k0 = gi * per_group_blocks + B_HEADS
        in_specs += [
            pl.BlockSpec((1, tq, w), lambda bi, h, i, q0=q0: (bi, i, q0 + h)),
            pl.BlockSpec((1, s, HEAD_DIM), lambda bi, h, i, k0=k0: (bi, 0, k0 + h)),
            pl.BlockSpec((1, s, HEAD_DIM), lambda bi, h, i, k0=k0: (bi, 0, k0 + B_KV_HEADS + h)),
        ]
        args += [proj, proj, proj]
    return pl.pallas_call(
        functools.partial(_dilated_kernel, configs=configs, seq=s, group=group),
        out_shape=jax.ShapeDtypeStruct((b, s, B_HEADS * HEAD_DIM), BF16),
        grid=(b, B_KV_HEADS, s // tq),
        in_specs=in_specs,
        out_specs=pl.BlockSpec((1, tq, w), lambda bi, h, i: (bi, i, h)),
        compiler_params=_params("parallel", "parallel", "arbitrary"),
        name="dilated_attention",
    )(*args)


def _differential_kernel(q_ref, k_ref, v_ref, lam_ref, sub_ref, o_ref, *, lambda_init, chains):
    lp = lam_ref[...]
    lam = (jnp.exp(jnp.sum(lp[0:1] * lp[1:2], axis=-1, keepdims=True))
           - jnp.exp(jnp.sum(lp[2:3] * lp[3:4], axis=-1, keepdims=True)) + lambda_init)
    k, v = k_ref[0], v_ref[0]
    rows = q_ref.shape[1] // chains
    def scores(c):
        q = q_ref[0, c * rows:(c + 1) * rows, :]
        return [_qk(q[:, mi * HEAD_DIM:(mi + 1) * HEAD_DIM], k[:, mi * HEAD_DIM:(mi + 1) * HEAD_DIM])
                for mi in range(2)]

    nxt = scores(0)
    for c in range(chains):
        cur = nxt
        if c + 1 < chains:
            nxt = scores(c + 1)
        probs = []
        for s in cur:
            m = jnp.max(s, axis=-1, keepdims=True)
            p = jnp.exp2(s - m)
            probs.append((p, jnp.sum(p, axis=-1, keepdims=True)))
        (p0, l0), (p1, l1) = probs
        a = p0 - p1 * (lam * l0 / l1)
        o = jnp.dot(a.astype(BF16), v, preferred_element_type=F32) / l0
        ms = jnp.mean(o * o, axis=-1, keepdims=True)
        o = o * lax.rsqrt(ms + EPS) * sub_ref[...] * (1.0 - lambda_init)
        o_ref[0, c * rows:(c + 1) * rows, :] = o.astype(o_ref.dtype)


def differential_attention(proj, lam_params, subln, lambda_init, *, tq=1024, chains=4):
    b, s, _ = proj.shape
    tq = _tile(s, tq, 16)
    w = 2 * HEAD_DIM
    return pl.pallas_call(
        functools.partial(_differential_kernel, lambda_init=lambda_init, chains=chains),
        out_shape=jax.ShapeDtypeStruct((b, s, C_HEADS * w), BF16),
        grid=(b, C_HEADS, s // tq),
        in_specs=[
            pl.BlockSpec((1, tq, w), lambda bi, h, i: (bi, i, h)),
            pl.BlockSpec((1, s, w), lambda bi, h, i: (bi, 0, C_HEADS + h)),
            pl.BlockSpec((1, s, w), lambda bi, h, i: (bi, 0, 2 * C_HEADS + h)),
            pl.BlockSpec((4, HEAD_DIM), lambda bi, h, i: (0, 0)),
            pl.BlockSpec((1, w), lambda bi, h, i: (0, 0)),
        ],
        out_specs=pl.BlockSpec((1, tq, w), lambda bi, h, i: (bi, i, h)),
        compiler_params=_params("parallel", "parallel", "arbitrary"),
        name="differential_attention",
    )(proj, proj, proj, lam_params.astype(F32), subln.reshape(1, w).astype(F32))


def _na_bias_kernel(rpb_ref, o_ref):
    rows = lax.broadcasted_iota(jnp.int32, (GRID_W, LANES), 0)
    lane = lax.broadcasted_iota(jnp.int32, (GRID_W, LANES), 1)
    n_rel = 2 * NA_ROWS - 1
    toeplitz = []
    for rho in range(n_rel):
        z = jnp.broadcast_to(rpb_ref[0, rho:rho + 1, :], (GRID_W, LANES))
        z = pltpu.roll(z, LANES - (NA_COLS - 1), 1)
        bit = 1
        while bit < GRID_W:
            z = jnp.where((rows & bit) != 0, pltpu.roll(z, bit, 1), z)
            bit *= 2
        toeplitz.append(z)
    kc = lane & (GRID_W - 1)
    first = jnp.clip(rows - NA_COLS // 2, 0, GRID_W - NA_COLS)
    inside = (kc >= first) & (kc < first + NA_COLS)
    for d in range(NA_ROWS):
        for pair in range(NA_ROWS // 2):
            even = toeplitz[d + 2 * pair]
            odd = pltpu.roll(toeplitz[d + 2 * pair + 1], GRID_W, 1)
            tile = jnp.where(lane < GRID_W, even, odd)
            o_ref[d, 0, :, pair * LANES:(pair + 1) * LANES] = jnp.where(inside, tile * LOG2E, MASKED)


def na_bias_table(rpb):
    h = rpb.shape[0]
    padded = jnp.zeros((h, 2 * NA_ROWS, LANES), F32).at[:, :2 * NA_ROWS - 1, :2 * NA_COLS - 1].set(rpb.astype(F32))
    return pl.pallas_call(
        _na_bias_kernel,
        out_shape=jax.ShapeDtypeStruct((NA_ROWS, h, GRID_W, NA_ROWS * GRID_W), F32),
        grid=(h,),
        in_specs=[pl.BlockSpec((1, 2 * NA_ROWS, LANES), lambda i: (i, 0, 0))],
        out_specs=pl.BlockSpec((NA_ROWS, 1, GRID_W, NA_ROWS * GRID_W), lambda i: (0, i, 0, 0)),
        compiler_params=_params("parallel"),
        name="na_bias_table",
    )(padded)


def _neighbourhood_kernel(q_ref, k_ref, v_ref, b_ref, o_ref, *, n_rows, group):
    win = NA_ROWS * GRID_W

    def row(r, carry):
        first = jnp.clip(r - NA_ROWS // 2, 0, n_rows - NA_ROWS)
        q0 = pl.multiple_of(r * GRID_W, GRID_W)
        k0 = pl.multiple_of(first * GRID_W, GRID_W)
        q = _stack_heads(q_ref[0, pl.ds(q0, GRID_W), :], group)
        s = _qk(q, k_ref[0, pl.ds(k0, win), :])
        s = (s.reshape(group, GRID_W, win) + b_ref[first - r + NA_ROWS - 1]).reshape(group * GRID_W, win)
        m = jnp.max(s, axis=-1, keepdims=True)
        p = jnp.exp2(s - m)
        l = jnp.sum(p, axis=-1, keepdims=True)
        o = jnp.dot(p.astype(BF16), v_ref[0, pl.ds(k0, win), :], preferred_element_type=F32) / l
        o_ref[0, pl.ds(q0, GRID_W), :] = _unstack_heads(o, group).astype(o_ref.dtype)
        return carry

    lax.fori_loop(0, n_rows, row, 0, unroll=8)


def neighbourhood_attention(proj, bias, *, kv_heads, group):
    b, s, _ = proj.shape
    n_rows = s // GRID_W
    assert n_rows >= NA_ROWS, "the key window is NA_ROWS grid rows tall"
    w = group * HEAD_DIM
    k_block0 = kv_heads * group
    v_block0 = k_block0 + kv_heads
    return pl.pallas_call(
        functools.partial(_neighbourhood_kernel, n_rows=n_rows, group=group),
        out_shape=jax.ShapeDtypeStruct((b, s, kv_heads * w), BF16),
        grid=(b, kv_heads),
        in_specs=[
            pl.BlockSpec((1, s, w), lambda bi, g: (bi, 0, g)),
            pl.BlockSpec((1, s, HEAD_DIM), lambda bi, g: (bi, 0, k_block0 + g)),
            pl.BlockSpec((1, s, HEAD_DIM), lambda bi, g: (bi, 0, v_block0 + g)),
            pl.BlockSpec((NA_ROWS, group, GRID_W, NA_ROWS * GRID_W), lambda bi, g: (0, g, 0, 0)),
        ],
        out_specs=pl.BlockSpec((1, s, w), lambda bi, g: (bi, 0, g)),
        compiler_params=_params("parallel", "arbitrary"),
        name="neighbourhood_attention",
    )(proj, proj, proj, bias)


IN_TILE = 1024


def _out_projection(o, x, w_out, layer, name):
    return matmul(o.reshape(-1, o.shape[-1]), w_out, layer=layer, out_dtype=F32, res=x, tk=o.shape[-1], name=name)


def mixer_axial_gqa(hn, x, w_in, w_out, layer, q_gain, k_gain, b, s):
    segments = [(q_gain * Q_SCALE, A_HEADS), (k_gain, A_KV_HEADS), (None, A_KV_HEADS)]
    tiles = (A_HEADS + 2 * A_KV_HEADS) * HEAD_DIM // IN_TILE
    proj = in_projection(hn, w_in, layer, segments, axial_rope_tables(s), s, tiles,
                         (A_HEADS + A_KV_HEADS) * HEAD_DIM // IN_TILE, tn=IN_TILE, name="a_in")
    o = dense_gqa(proj.reshape(b, s, -1), kv_heads=A_KV_HEADS, group=A_HEADS // A_KV_HEADS)
    return _out_projection(o, x, w_out, layer, "a_out")


def mixer_dilated(hn, x, w_in, w_out, layer, q_gain, k_gain, b, s):
    segments = []
    for gi in range(len(B_CONFIGS)):
        segments += [(q_gain[gi] * Q_SCALE, B_HEADS), (k_gain[gi], B_KV_HEADS), (None, B_KV_HEADS)]
    tn = B_KV_HEADS * HEAD_DIM
    tiles = (B_HEADS + 2 * B_KV_HEADS) * HEAD_DIM // tn
    proj = in_projection(hn, w_in, layer, segments, full_rope_tables(s), s, tiles, tiles - 1, tn=tn, name="b_in")
    o = dilated_attention(proj.reshape(b, s, -1))
    return _out_projection(o, x, w_out, layer, "b_out")


def mixer_differential(hn, x, w_in, w_out, layer, q_gain, k_gain, lam_params, subln, lambda_init, b, s):
    segments = [(q_gain * Q_SCALE, 2 * C_HEADS), (k_gain, 2 * C_HEADS), (None, 2 * C_HEADS)]
    tiles = 3 * 2 * C_HEADS * HEAD_DIM // IN_TILE
    proj = in_projection(hn, w_in, layer, segments, full_rope_tables(s), s, tiles, 2 * tiles // 3,
                         tn=IN_TILE, name="c_in")
    o = differential_attention(proj.reshape(b, s, -1), lam_params, subln, lambda_init)
    return _out_projection(o, x, w_out, layer, "c_out")


def mixer_neighbourhood(hn, x, w_in, w_out, layer, q_gain, k_gain, rpb, b, s):
    segments = [(q_gain * Q_SCALE, D_HEADS), (k_gain, D_KV_HEADS), (None, D_KV_HEADS)]
    tiles = (D_HEADS + 2 * D_KV_HEADS) * HEAD_DIM // IN_TILE
    proj = in_projection(hn, w_in, layer, segments, None, s, tiles,
                         (D_HEADS + D_KV_HEADS) * HEAD_DIM // IN_TILE, tn=IN_TILE, name="d_in")
    o = neighbourhood_attention(proj.reshape(b, s, -1), na_bias_table(rpb),
                                kv_heads=D_KV_HEADS, group=D_HEADS // D_KV_HEADS)
    return _out_projection(o, x, w_out, layer, "d_out")


def squared_relu_mlp(x, gain, w_up, w_down, layer, split=None):
    hn = rmsnorm(x, gain)
    u = matmul(hn, w_up, layer=layer, out_dtype=BF16, act="relu2", tk=hn.shape[1], name="mlp_up")
    down = functools.partial(matmul, u, w_down, layer=layer, out_dtype=F32, res=x, tk=2048, name="mlp_down")
    if split is None:
        return down()
    return down(rows=(0, split)), down(rows=(split, u.shape[0] - split))


def trunk(x, b, s, ln_mix, ln_ffn, w_up, w_down, mixers, split):
    depth = ln_mix.shape[0]
    for i in range(depth):
        m, j = i % N_MIXERS, i // N_MIXERS
        hn = rmsnorm(x, ln_mix[i])
        p = mixers[m]
        common = (hn, x, p["w_in"], p["w_out"], j, p["q_gain"][j].astype(F32), p["k_gain"][j].astype(F32))
        if m == 0:
            x = mixer_axial_gqa(*common, b, s)
        elif m == 1:
            x = mixer_dilated(*common, b, s)
        elif m == 2:
            lambda_init = 0.8 - 0.6 * math.exp(-0.3 * i)
            x = mixer_differential(*common, p["lam"][j], p["subln"][j], lambda_init, b, s)
        else:
            x = mixer_neighbourhood(*common, p["rpb"][j], b, s)
        x = squared_relu_mlp(x, ln_ffn[i], w_up, w_down, i, split=split if i == depth - 1 else None)
    return x


def kernel(x_prompt, x_sample, ln_mix, ln_ffn, w_up, w_down, a_w_in, a_w_out, a_q_gain, a_k_gain, b_w_in, b_w_out, b_q_gain, b_k_gain, c_w_in, c_w_out, c_q_gain, c_k_gain, c_lambda, c_subln, d_w_in, d_w_out, d_q_gain, d_k_gain, d_rpb):
    bp, s, d = x_prompt.shape
    bs = x_sample.shape[0]
    assert x_sample.shape[1:] == (s, d)
    b = bp + bs
    x = (x_prompt.reshape(bp * s, d), x_sample.reshape(bs * s, d))
    cast = lambda w: w.astype(BF16)
    mixers = [
        dict(w_in=cast(a_w_in), w_out=cast(a_w_out), q_gain=a_q_gain, k_gain=a_k_gain),
        dict(w_in=cast(b_w_in), w_out=cast(b_w_out), q_gain=b_q_gain, k_gain=b_k_gain),
        dict(w_in=cast(c_w_in), w_out=cast(c_w_out), q_gain=c_q_gain, k_gain=c_k_gain,
             lam=c_lambda, subln=c_subln),
        dict(w_in=cast(d_w_in), w_out=cast(d_w_out), q_gain=d_q_gain, k_gain=d_k_gain, rpb=d_rpb),
    ]
    y_prompt, y_sample = trunk(x, b, s, ln_mix, ln_ffn, cast(w_up), cast(w_down), mixers, split=bp * s)
    return (y_prompt.reshape(bp, s, d), y_sample.reshape(bs, s, d))
```

```python
import functools
import math

import jax
import jax.numpy as jnp
from jax import lax
from jax.experimental import pallas as pl
from jax.experimental.pallas import tpu as pltpu

F32 = jnp.float32
BF16 = jnp.bfloat16

HEAD_DIM = 128
GRID_W = 64
ROPE_THETA = 10000.0
EPS = 1e-6
LOG2E = math.log2(math.e)
Q_SCALE = HEAD_DIM ** -0.5 * LOG2E
N_MIXERS = 4
A_HEADS, A_KV_HEADS = 32, 8
B_CONFIGS = ((128, 1), (512, 4), (2048, 16))
B_HEADS, B_KV_HEADS = 16, 4
C_HEADS = 16
D_HEADS, D_KV_HEADS = 32, 8
NA_ROWS, NA_COLS = 8, 16
MASKED = -1e30
RESIDUE_LIFT_BITS = 16

LANES = 128
V7X_VMEM_BYTES = 64 * 1024 * 1024
VMEM_LIMIT_BYTES = V7X_VMEM_BYTES - 8 * 1024 * 1024
MATMUL_TEMP_BYTES = 6 * 1024 * 1024


def _tile(dim, pref, quantum):
    if dim <= pref:
        return dim
    t = (pref // quantum) * quantum
    while dim % t:
        t -= quantum
    return t


def _params(*semantics):
    return pltpu.CompilerParams(dimension_semantics=semantics, vmem_limit_bytes=VMEM_LIMIT_BYTES)


def _row_parts(x):
    return x if isinstance(x, tuple) else (x,)


def _row_part_specs(parts, tile_rows, width, row_tile, col_tile):
    specs, first = [], 0
    for p in parts:
        n = p.shape[0] // tile_rows
        specs.append(pl.BlockSpec(
            (tile_rows, width),
            lambda *g, first=first, n=n: (jnp.clip(row_tile(*g) - first, 0, n - 1), col_tile(*g))))
        first += n
    return specs


def _read_part(refs, part_tiles, row_tile, read):
    value, first = None, 0
    for ref, n in zip(refs, part_tiles):
        value = read(ref) if value is None else jnp.where(row_tile >= first, read(ref), value)
        first += n
    return value


def _rmsnorm_kernel(*refs, part_tiles):
    *x_refs, g_ref, o_ref = refs

    def emit(x_ref):
        x = x_ref[...]
        ms = jnp.mean(x * x, axis=-1, keepdims=True)
        o_ref[...] = (x * lax.rsqrt(ms + EPS) * g_ref[...]).astype(o_ref.dtype)

    first = 0
    for x_ref, n in zip(x_refs, part_tiles):
        i = pl.program_id(0)
        pl.when((i >= first) & (i < first + n))(functools.partial(emit, x_ref))
        first += n


def rmsnorm(x, gain):
    parts = _row_parts(x)
    d = parts[0].shape[1]
    t = sum(p.shape[0] for p in parts)
    tr = _tile(math.gcd(*[p.shape[0] for p in parts]), 512, 8)
    return pl.pallas_call(
        functools.partial(_rmsnorm_kernel, part_tiles=[p.shape[0] // tr for p in parts]),
        out_shape=jax.ShapeDtypeStruct((t, d), BF16),
        grid=(t // tr,),
        in_specs=_row_part_specs(parts, tr, d, lambda i: i, lambda i: 0) + [pl.BlockSpec((1, d), lambda i: (0, 0))],
        out_specs=pl.BlockSpec((tr, d), lambda i: (i, 0)),
        compiler_params=_params("parallel"),
        name="rmsnorm",
    )(*parts, gain.reshape(1, d).astype(F32))


def _lane_partial_sums(v):
    return functools.reduce(lambda a, b: a + b, [v[:, c:c + LANES] for c in range(0, v.shape[1], LANES)])


def _matmul_kernel(a_ref, w_ref, *refs, nk, act, res_tiles, first_tile, chunk, norm_in, norm_out, norm_dim):
    refs = list(refs)
    n_out = 3 if norm_out else 1
    o_ref, *norm_out_refs = refs[len(refs) - n_out:]
    refs = refs[:len(refs) - n_out]
    gain_ref = refs.pop() if norm_out else None
    sq_ref = refs.pop() if norm_in else None
    res_refs = refs
    k = pl.program_id(2)

    def residual(cols):
        return _read_part(res_refs, res_tiles, first_tile + pl.program_id(0), lambda ref: ref[:, cols])

    def emit_norm(x, cols, first):
        xg_ref, sq_out_ref = norm_out_refs
        xg_ref[:, cols] = (x * gain_ref[:, cols]).astype(xg_ref.dtype)
        sq = _lane_partial_sums(x * x)
        sq_out_ref[...] = sq if first else sq_out_ref[...] + sq

    if norm_in:
        scale = lax.rsqrt(jnp.sum(sq_ref[...], axis=-1, keepdims=True) * (1.0 / norm_dim) + EPS)

    if nk > 1:
        @pl.when(k == 0)
        def _():
            o_ref[...] = residual(slice(None)) if res_refs else jnp.zeros_like(o_ref)

    for c in range(o_ref.shape[1] // chunk):
        cols = slice(c * chunk, (c + 1) * chunk)
        part = jnp.dot(a_ref[...], w_ref[:, cols], preferred_element_type=F32)
        if nk > 1:
            o_ref[:, cols] += part
            continue
        if norm_in:
            part = part * scale
        if res_refs:
            part = part + residual(cols)
        if act == "relu2":
            r = jnp.maximum(part, 0.0)
            part = r * r
        o_ref[:, cols] = part.astype(o_ref.dtype)
        if norm_out:
            emit_norm(part, cols, c == 0)

    if nk > 1 and norm_out:
        @pl.when(k == nk - 1)
        def _():
            for c in range(o_ref.shape[1] // chunk):
                cols = slice(c * chunk, (c + 1) * chunk)
                emit_norm(o_ref[:, cols], cols, c == 0)


def matmul(a, w, *, out_dtype, layer=None, act=None, res=None, rows=None, norm_in=None, norm_out=None,
           tm=1024, tn=1024, tk=2048, chunk=256, name="matmul"):
    row0, m = rows if rows is not None else (0, a.shape[0])
    kdim, n = w.shape[-2:]
    res_parts = _row_parts(res) if res is not None else ()
    tm = _tile(math.gcd(m, row0, *[p.shape[0] for p in res_parts[:-1]]), tm, 8)
    tn, tk = _tile(n, tn, LANES), _tile(kdim, tk, LANES)

    def window_bytes(tk):
        per_step = tm * tk * 2 + tk * tn * 2 + tm * tn * (4 * len(res_parts) + jnp.dtype(out_dtype).itemsize)
        if norm_out is not None:
            per_step += tm * tn * 2 + tm * LANES * 4
        if norm_in is not None:
            per_step += tm * norm_in[0].shape[1] * 4
        return 2 * per_step

    while window_bytes(tk) > VMEM_LIMIT_BYTES - MATMUL_TEMP_BYTES and tk % (2 * LANES) == 0:
        tk //= 2
    nk = kdim // tk
    t0 = row0 // tm
    chunk = min(chunk, tn)
    assert nk == 1 or (out_dtype == F32 and act is None), "a split K accumulates in the f32 output block"
    if w.ndim == 3:
        w_spec = pl.BlockSpec((None, tk, tn), lambda i, j, k: (layer, k, j))
    else:
        w_spec = pl.BlockSpec((tk, tn), lambda i, j, k: (k, j))
    in_specs = [pl.BlockSpec((tm, tk), lambda i, j, k: (t0 + i, k)), w_spec]
    args = [a, w]
    in_specs += _row_part_specs(res_parts, tm, tn, lambda i, j, k: t0 + i, lambda i, j, k: j)
    args += res_parts
    norm_dim = None
    if norm_in is not None:
        sq, norm_dim = norm_in
        assert nk == 1, "the row scale is applied where the whole product is formed"
        in_specs.append(pl.BlockSpec((tm, sq.shape[1]), lambda i, j, k: (t0 + i, 0)))
        args.append(sq)
    out_shape = jax.ShapeDtypeStruct((m, n), out_dtype)
    out_specs = pl.BlockSpec((tm, tn), lambda i, j, k: (i, j))
    if norm_out is not None:
        assert out_dtype == F32
        in_specs.append(pl.BlockSpec((1, tn), lambda i, j, k: (0, j)))
        args.append(norm_out.reshape(1, n).astype(F32))
        out_shape = (out_shape, jax.ShapeDtypeStruct((m, n), BF16),
                     jax.ShapeDtypeStruct((m, n // tn * LANES), F32))
        out_specs = (out_specs, pl.BlockSpec((tm, tn), lambda i, j, k: (i, j)),
                     pl.BlockSpec((tm, LANES), lambda i, j, k: (i, j)))
    return pl.pallas_call(
        functools.partial(_matmul_kernel, nk=nk, act=act, chunk=chunk, first_tile=t0,
                          res_tiles=[p.shape[0] // tm for p in res_parts],
                          norm_in=norm_in is not None, norm_out=norm_out is not None, norm_dim=norm_dim),
        out_shape=out_shape,
        grid=(m // tm, n // tn, nk),
        in_specs=in_specs,
        out_specs=out_specs,
        compiler_params=_params("parallel", "parallel", "arbitrary"),
        name=name,
    )(*args)


def rope_tables(pos, dim):
    inv = ROPE_THETA ** (-jnp.arange(0, dim, 2, dtype=F32) / dim)
    ang = pos.astype(F32)[:, None] * inv[None, :]
    ang = jnp.concatenate([ang, ang], axis=-1)
    return jnp.cos(ang), jnp.sin(ang)


def full_rope_tables(s):
    cos, sin = rope_tables(jnp.arange(s), HEAD_DIM)
    lane = jnp.arange(HEAD_DIM)
    sin_signed = jnp.where(lane < HEAD_DIM // 2, -sin, sin)
    return ((HEAD_DIM // 2,), (cos, sin_signed))


def axial_rope_tables(s):
    half = HEAD_DIM // 2
    t = jnp.arange(s)
    cr, sr = rope_tables(t // GRID_W, half)
    cc, sc = rope_tables(t % GRID_W, half)
    cos = jnp.concatenate([cr, cc], axis=-1)
    sin = jnp.concatenate([sr, sc], axis=-1)
    low = (jnp.arange(HEAD_DIM) % half) < half // 2
    return ((HEAD_DIM - half // 2, half // 2),
            (cos, jnp.where(low, -sin, 0.0), jnp.where(low, 0.0, sin)))


def _prep_kernel(x_ref, gain_ref, *refs, shifts):
    *tab_refs, o_ref = refs
    tables = [t[...] for t in tab_refs]
    averager = jnp.full((HEAD_DIM, HEAD_DIM), 1.0 / HEAD_DIM, BF16)
    for h0 in range(0, x_ref.shape[1], HEAD_DIM):
        head = slice(h0, h0 + HEAD_DIM)
        x = x_ref[:, head].astype(F32)
        ms = jnp.dot((x * x).astype(BF16), averager, preferred_element_type=F32)
        y = x * lax.rsqrt(ms + EPS) * gain_ref[:, head]
        if shifts:
            r = y * tables[0]
            for sh, t in zip(shifts, tables[1:]):
                r = r + pltpu.roll(y, sh, 1) * t
            y = r
        o_ref[:, head] = y.astype(o_ref.dtype)


def in_projection(normed, w_in, layer, segments, rope, seq, tiles_per_group, v_tile, *, tn, name):
    gains = jnp.concatenate([jnp.tile(jnp.ones((HEAD_DIM,), F32) if g is None else g.astype(F32), n)
                             for g, n in segments])[None]
    shifts, tables = rope if rope is not None else ((), ())
    a, norm_in = normed
    proj = matmul(a, w_in, layer=layer, out_dtype=BF16, norm_in=norm_in, tk=a.shape[1], name=name)
    t, n = proj.shape
    ts = _tile(seq, 1024, 16)
    per_seq = seq // ts
    qk_tiles = n // tn // tiles_per_group * v_tile
    tile = lambda j: j // v_tile * tiles_per_group + j % v_tile
    return pl.pallas_call(
        functools.partial(_prep_kernel, shifts=shifts),
        out_shape=jax.ShapeDtypeStruct((t, n), BF16),
        grid=(t // ts, qk_tiles),
        in_specs=[pl.BlockSpec((ts, tn), lambda i, j: (i, tile(j))), pl.BlockSpec((1, tn), lambda i, j: (0, tile(j)))]
        + [pl.BlockSpec((ts, HEAD_DIM), lambda i, j: (i % per_seq, 0))] * len(tables),
        out_specs=pl.BlockSpec((ts, tn), lambda i, j: (i, tile(j))),
        input_output_aliases={0: 0},
        compiler_params=_params("parallel", "parallel"),
        name=name + "_prep",
    )(proj, gains, *[tab.astype(F32) for tab in tables])


def _stack_heads(x, n):
    return jnp.concatenate([x[:, h * HEAD_DIM:(h + 1) * HEAD_DIM] for h in range(n)], axis=0)


def _unstack_heads(x, n):
    rows = x.shape[0] // n
    return jnp.concatenate([x[h * rows:(h + 1) * rows] for h in range(n)], axis=1)


def _qk(q, k):
    return lax.dot_general(q, k, (((1,), (1,)), ((), ())), preferred_element_type=F32)


def _dense_gqa_kernel(q_ref, k_ref, v_ref, o_ref, *, group, chains):
    k, v = k_ref[0], v_ref[0]
    per = group // chains

    def scores(c):
        return _qk(_stack_heads(q_ref[0, :, c * per * HEAD_DIM:(c + 1) * per * HEAD_DIM], per), k)

    ahead = [scores(c) for c in range(min(2, chains))]
    for c in range(chains):
        cols = slice(c * per * HEAD_DIM, (c + 1) * per * HEAD_DIM)
        s = ahead.pop(0)
        if c + 2 < chains:
            ahead.append(scores(c + 2))
        m = jnp.max(s, axis=-1, keepdims=True)
        p = jnp.exp2(s - m)
        l = jnp.sum(p, axis=-1, keepdims=True)
        o = jnp.dot(p.astype(BF16), v, preferred_element_type=F32) / l
        o_ref[0, :, cols] = _unstack_heads(o, per).astype(o_ref.dtype)


def dense_gqa(proj, *, kv_heads, group, tq=256, chains=4):
    b, s, _ = proj.shape
    tq = _tile(s, tq, 16)
    w = group * HEAD_DIM
    k_block0 = kv_heads * group
    v_block0 = k_block0 + kv_heads
    return pl.pallas_call(
        functools.partial(_dense_gqa_kernel, group=group, chains=chains),
        out_shape=jax.ShapeDtypeStruct((b, s, kv_heads * w), BF16),
        grid=(b, kv_heads, s // tq),
        in_specs=[
            pl.BlockSpec((1, tq, w), lambda bi, g, i: (bi, i, g)),
            pl.BlockSpec((1, s, HEAD_DIM), lambda bi, g, i: (bi, 0, k_block0 + g)),
            pl.BlockSpec((1, s, HEAD_DIM), lambda bi, g, i: (bi, 0, v_block0 + g)),
        ],
        out_specs=pl.BlockSpec((1, tq, w), lambda bi, g, i: (bi, i, g)),
        compiler_params=_params("parallel", "parallel", "arbitrary"),
        name="dense_gqa",
    )(proj, proj, proj)


def _dilated_kernel(*refs, configs, seq, group):
    o_ref = refs[-1]
    i = pl.program_id(2)
    tq = o_ref.shape[1]

    chains = []
    for gi, (window, dil) in enumerate(configs):
        reach = (window // (2 * dil)) * dil
        pad = -(-reach // tq) * tq
        wlen = min(seq, tq + 2 * pad)
        start = pl.multiple_of(jnp.clip(i * tq - pad, 0, seq - wlen), tq)
        u = (start + lax.broadcasted_iota(jnp.int32, (tq, wlen), 1)
             - i * tq - lax.broadcasted_iota(jnp.int32, (tq, wlen), 0) + reach)
        if dil > 1:
            u = u + ((u & (dil - 1)) << RESIDUE_LIFT_BITS)
        ok = lax.bitcast_convert_type(u, jnp.uint32) <= jnp.uint32(2 * reach)
        bias = jnp.where(ok, 0.0, MASKED)
        per = group if wlen <= 4 * tq else max(1, group // 2)
        for h0 in range(0, group, per):
            chains.append((gi, h0, per, start, wlen, bias))

    def scores(c):
        gi, h0, per, start, wlen, _ = chains[c]
        q_ref, k_ref = refs[3 * gi], refs[3 * gi + 1]
        q = _stack_heads(q_ref[0, :, h0 * HEAD_DIM:(h0 + per) * HEAD_DIM], per)
        return _qk(q, k_ref[0, pl.ds(start, wlen), :])

    outs = [[None] * group for _ in configs]
    lses = [[None] * group for _ in configs]
    ahead = [scores(c) for c in range(min(2, len(chains)))]
    for c, (gi, h0, per, start, wlen, bias) in enumerate(chains):
        s = ahead.pop(0)
        if c + 2 < len(chains):
            ahead.append(scores(c + 2))
        s = (s.reshape(per, tq, wlen) + bias[None]).reshape(per * tq, wlen)
        m = jnp.max(s, axis=-1, keepdims=True)
        p = jnp.exp2(s - m)
        l = jnp.sum(p, axis=-1, keepdims=True)
        v_ref = refs[3 * gi + 2]
        o = jnp.dot(p.astype(BF16), v_ref[0, pl.ds(start, wlen), :], preferred_element_type=F32) / l
        lse = m + jnp.log2(l)
        for h in range(per):
            outs[gi][h0 + h] = o[h * tq:(h + 1) * tq]
            lses[gi][h0 + h] = lse[h * tq:(h + 1) * tq]

    for h in range(group):
        lse_h = [lses[gi][h] for gi in range(len(configs))]
        top = functools.reduce(jnp.maximum, lse_h)
        ws = [jnp.exp2(x - top) for x in lse_h]
        o = sum(w * outs[gi][h] for gi, w in enumerate(ws)) / sum(ws)
        o_ref[0, :, h * HEAD_DIM:(h + 1) * HEAD_DIM] = o.astype(o_ref.dtype)


def dilated_attention(proj, *, configs=B_CONFIGS, tq=256):
    b, s, _ = proj.shape
    tq = _tile(s, tq, LANES)
    for window, dil in configs:
        assert dil & (dil - 1) == 0, "the residue test uses a power-of-two dilation"
        assert 2 * window + s < 1 << RESIDUE_LIFT_BITS, "a lifted residue must clear every in-range offset"
    group = B_HEADS // B_KV_HEADS
    w = group * HEAD_DIM
    per_group_blocks = B_HEADS + 2 * B_KV_HEADS
    in_specs, args = [], []
    for gi in range(len(configs)):
        q0 = gi * per_group_blocks // group
        k0 = gi * per_group_blocks + B_HEADS
        in_specs += [
            pl.BlockSpec((1, tq, w), lambda bi, h, i, q0=q0: (bi, i, q0 + h)),
            pl.BlockSpec((1, s, HEAD_DIM), lambda bi, h, i, k0=k0: (bi, 0, k0 + h)),
            pl.BlockSpec((1, s, HEAD_DIM), lambda bi, h, i, k0=k0: (bi, 0, k0 + B_KV_HEADS + h)),
        ]
        args += [proj, proj, proj]
    return pl.pallas_call(
        functools.partial(_dilated_kernel, configs=configs, seq=s, group=group),
        out_shape=jax.ShapeDtypeStruct((b, s, B_HEADS * HEAD_DIM), BF16),
        grid=(b, B_KV_HEADS, s // tq),
        in_specs=in_specs,
        out_specs=pl.BlockSpec((1, tq, w), lambda bi, h, i: (bi, i, h)),
        compiler_params=_params("parallel", "parallel", "arbitrary"),
        name="dilated_attention",
    )(*args)


def _differential_kernel(q_ref, k_ref, v_ref, lam_ref, sub_ref, o_ref, *, lambda_init, chains):
    lp = lam_ref[...]
    lam = (jnp.exp(jnp.sum(lp[0:1] * lp[1:2], axis=-1, keepdims=True))
           - jnp.exp(jnp.sum(lp[2:3] * lp[3:4], axis=-1, keepdims=True)) + lambda_init)
    k, v = k_ref[0], v_ref[0]
    rows = q_ref.shape[1] // chains
    def scores(c):
        q = q_ref[0, c * rows:(c + 1) * rows, :]
        return [_qk(q[:, mi * HEAD_DIM:(mi + 1) * HEAD_DIM], k[:, mi * HEAD_DIM:(mi + 1) * HEAD_DIM])
                for mi in range(2)]

    nxt = scores(0)
    for c in range(chains):
        cur = nxt
        if c + 1 < chains:
            nxt = scores(c + 1)
        probs = []
        for s in cur:
            m = jnp.max(s, axis=-1, keepdims=True)
            p = jnp.exp2(s - m)
            probs.append((p, jnp.sum(p, axis=-1, keepdims=True)))
        (p0, l0), (p1, l1) = probs
        a = p0 - p1 * (lam * l0 / l1)
        o = jnp.dot(a.astype(BF16), v, preferred_element_type=F32) / l0
        ms = jnp.mean(o * o, axis=-1, keepdims=True)
        o = o * lax.rsqrt(ms + EPS) * sub_ref[...] * (1.0 - lambda_init)
        o_ref[0, c * rows:(c + 1) * rows, :] = o.astype(o_ref.dtype)


def differential_attention(proj, lam_params, subln, lambda_init, *, tq=1024, chains=4):
    b, s, _ = proj.shape
    tq = _tile(s, tq, 16)
    w = 2 * HEAD_DIM
    return pl.pallas_call(
        functools.partial(_differential_kernel, lambda_init=lambda_init, chains=chains),
        out_shape=jax.ShapeDtypeStruct((b, s, C_HEADS * w), BF16),
        grid=(b, C_HEADS, s // tq),
        in_specs=[
            pl.BlockSpec((1, tq, w), lambda bi, h, i: (bi, i, h)),
            pl.BlockSpec((1, s, w), lambda bi, h, i: (bi, 0, C_HEADS + h)),
            pl.BlockSpec((1, s, w), lambda bi, h, i: (bi, 0, 2 * C_HEADS + h)),
            pl.BlockSpec((4, HEAD_DIM), lambda bi, h, i: (0, 0)),
            pl.BlockSpec((1, w), lambda bi, h, i: (0, 0)),
        ],
        out_specs=pl.BlockSpec((1, tq, w), lambda bi, h, i: (bi, i, h)),
        compiler_params=_params("parallel", "parallel", "arbitrary"),
        name="differential_attention",
    )(proj, proj, proj, lam_params.astype(F32), subln.reshape(1, w).astype(F32))


def _na_bias_kernel(rpb_ref, o_ref):
    rows = lax.broadcasted_iota(jnp.int32, (GRID_W, LANES), 0)
    lane = lax.broadcasted_iota(jnp.int32, (GRID_W, LANES), 1)
    n_rel = 2 * NA_ROWS - 1
    toeplitz = []
    for rho in range(n_rel):
        z = jnp.broadcast_to(rpb_ref[0, rho:rho + 1, :], (GRID_W, LANES))
        z = pltpu.roll(z, LANES - (NA_COLS - 1), 1)
        bit = 1
        while bit < GRID_W:
            z = jnp.where((rows & bit) != 0, pltpu.roll(z, bit, 1), z)
            bit *= 2
        toeplitz.append(z)
    kc = lane & (GRID_W - 1)
    first = jnp.clip(rows - NA_COLS // 2, 0, GRID_W - NA_COLS)
    inside = (kc >= first) & (kc < first + NA_COLS)
    for d in range(NA_ROWS):
        for pair in range(NA_ROWS // 2):
            even = toeplitz[d + 2 * pair]
            odd = pltpu.roll(toeplitz[d + 2 * pair + 1], GRID_W, 1)
            tile = jnp.where(lane < GRID_W, even, odd)
            o_ref[d, 0, :, pair * LANES:(pair + 1) * LANES] = jnp.where(inside, tile * LOG2E, MASKED)


def na_bias_table(rpb):
    h = rpb.shape[0]
    padded = jnp.zeros((h, 2 * NA_ROWS, LANES), F32).at[:, :2 * NA_ROWS - 1, :2 * NA_COLS - 1].set(rpb.astype(F32))
    return pl.pallas_call(
        _na_bias_kernel,
        out_shape=jax.ShapeDtypeStruct((NA_ROWS, h, GRID_W, NA_ROWS * GRID_W), F32),
        grid=(h,),
        in_specs=[pl.BlockSpec((1, 2 * NA_ROWS, LANES), lambda i: (i, 0, 0))],
        out_specs=pl.BlockSpec((NA_ROWS, 1, GRID_W, NA_ROWS * GRID_W), lambda i: (0, i, 0, 0)),
        compiler_params=_params("parallel"),
        name="na_bias_table",
    )(padded)


def _neighbourhood_kernel(q_ref, k_ref, v_ref, b_ref, o_ref, *, n_rows, group):
    win = NA_ROWS * GRID_W

    def row(r, carry):
        first = jnp.clip(r - NA_ROWS // 2, 0, n_rows - NA_ROWS)
        q0 = pl.multiple_of(r * GRID_W, GRID_W)
        k0 = pl.multiple_of(first * GRID_W, GRID_W)
        q = _stack_heads(q_ref[0, pl.ds(q0, GRID_W), :], group)
        s = _qk(q, k_ref[0, pl.ds(k0, win), :])
        s = (s.reshape(group, GRID_W, win) + b_ref[first - r + NA_ROWS - 1]).reshape(group * GRID_W, win)
        m = jnp.max(s, axis=-1, keepdims=True)
        p = jnp.exp2(s - m)
        l = jnp.sum(p, axis=-1, keepdims=True)
        o = jnp.dot(p.astype(BF16), v_ref[0, pl.ds(k0, win), :], preferred_element_type=F32) / l
        o_ref[0, pl.ds(q0, GRID_W), :] = _unstack_heads(o, group).astype(o_ref.dtype)
        return carry

    lax.fori_loop(0, n_rows, row, 0, unroll=16)


def neighbourhood_attention(proj, bias, *, kv_heads, group):
    b, s, _ = proj.shape
    n_rows = s // GRID_W
    assert n_rows >= NA_ROWS, "the key window is NA_ROWS grid rows tall"
    w = group * HEAD_DIM
    k_block0 = kv_heads * group
    v_block0 = k_block0 + kv_heads
    return pl.pallas_call(
        functools.partial(_neighbourhood_kernel, n_rows=n_rows, group=group),
        out_shape=jax.ShapeDtypeStruct((b, s, kv_heads * w), BF16),
        grid=(b, kv_heads),
        in_specs=[
            pl.BlockSpec((1, s, w), lambda bi, g: (bi, 0, g)),
            pl.BlockSpec((1, s, HEAD_DIM), lambda bi, g: (bi, 0, k_block0 + g)),
            pl.BlockSpec((1, s, HEAD_DIM), lambda bi, g: (bi, 0, v_block0 + g)),
            pl.BlockSpec((NA_ROWS, group, GRID_W, NA_ROWS * GRID_W), lambda bi, g: (0, g, 0, 0)),
        ],
        out_specs=pl.BlockSpec((1, s, w), lambda bi, g: (bi, 0, g)),
        compiler_params=_params("parallel", "arbitrary"),
        name="neighbourhood_attention",
    )(proj, proj, proj, bias)


IN_TILE = 1024


def _with_next_norm(outputs):
    x, xg, sq = outputs
    return x, (xg, (sq, x.shape[1]))


def _out_projection(o, x, w_out, layer, next_gain, name):
    return _with_next_norm(matmul(o.reshape(-1, o.shape[-1]), w_out, layer=layer, out_dtype=F32, res=x,
                                  norm_out=next_gain, tk=o.shape[-1], name=name))


def mixer_axial_gqa(normed, x, w_in, w_out, layer, next_gain, q_gain, k_gain, b, s):
    segments = [(q_gain * Q_SCALE, A_HEADS), (k_gain, A_KV_HEADS), (None, A_KV_HEADS)]
    tiles = (A_HEADS + 2 * A_KV_HEADS) * HEAD_DIM // IN_TILE
    proj = in_projection(normed, w_in, layer, segments, axial_rope_tables(s), s, tiles,
                         (A_HEADS + A_KV_HEADS) * HEAD_DIM // IN_TILE, tn=IN_TILE, name="a_in")
    o = dense_gqa(proj.reshape(b, s, -1), kv_heads=A_KV_HEADS, group=A_HEADS // A_KV_HEADS)
    return _out_projection(o, x, w_out, layer, next_gain, "a_out")


def mixer_dilated(normed, x, w_in, w_out, layer, next_gain, q_gain, k_gain, b, s):
    segments = []
    for gi in range(len(B_CONFIGS)):
        segments += [(q_gain[gi] * Q_SCALE, B_HEADS), (k_gain[gi], B_KV_HEADS), (None, B_KV_HEADS)]
    tn = B_KV_HEADS * HEAD_DIM
    tiles = (B_HEADS + 2 * B_KV_HEADS) * HEAD_DIM // tn
    proj = in_projection(normed, w_in, layer, segments, full_rope_tables(s), s, tiles, tiles - 1, tn=tn, name="b_in")
    o = dilated_attention(proj.reshape(b, s, -1))
    return _out_projection(o, x, w_out, layer, next_gain, "b_out")


def mixer_differential(normed, x, w_in, w_out, layer, next_gain, q_gain, k_gain, lam_params, subln, lambda_init,
                       b, s):
    segments = [(q_gain * Q_SCALE, 2 * C_HEADS), (k_gain, 2 * C_HEADS), (None, 2 * C_HEADS)]
    tiles = 3 * 2 * C_HEADS * HEAD_DIM // IN_TILE
    proj = in_projection(normed, w_in, layer, segments, full_rope_tables(s), s, tiles, 2 * tiles // 3,
                         tn=IN_TILE, name="c_in")
    o = differential_attention(proj.reshape(b, s, -1), lam_params, subln, lambda_init)
    return _out_projection(o, x, w_out, layer, next_gain, "c_out")


def mixer_neighbourhood(normed, x, w_in, w_out, layer, next_gain, q_gain, k_gain, rpb, b, s):
    segments = [(q_gain * Q_SCALE, D_HEADS), (k_gain, D_KV_HEADS), (None, D_KV_HEADS)]
    tiles = (D_HEADS + 2 * D_KV_HEADS) * HEAD_DIM // IN_TILE
    proj = in_projection(normed, w_in, layer, segments, None, s, tiles,
                         (D_HEADS + D_KV_HEADS) * HEAD_DIM // IN_TILE, tn=IN_TILE, name="d_in")
    o = neighbourhood_attention(proj.reshape(b, s, -1), na_bias_table(rpb),
                                kv_heads=D_KV_HEADS, group=D_HEADS // D_KV_HEADS)
    return _out_projection(o, x, w_out, layer, next_gain, "d_out")


def squared_relu_mlp(normed, x, w_up, w_down, layer, next_gain=None, split=None):
    a, norm_in = normed
    u = matmul(a, w_up, layer=layer, out_dtype=BF16, act="relu2", norm_in=norm_in, tk=a.shape[1], name="mlp_up")
    down = functools.partial(matmul, u, w_down, layer=layer, out_dtype=F32, res=x, tk=4096, name="mlp_down")
    if split is None:
        return _with_next_norm(down(norm_out=next_gain))
    return down(rows=(0, split)), down(rows=(split, u.shape[0] - split))


def trunk(x, b, s, ln_mix, ln_ffn, w_up, w_down, mixers, split):
    depth = ln_mix.shape[0]
    normed = (rmsnorm(x, ln_mix[0]), None)
    for i in range(depth):
        m, j = i % N_MIXERS, i // N_MIXERS
        p = mixers[m]
        common = (normed, x, p["w_in"], p["w_out"], j, ln_ffn[i],
                  p["q_gain"][j].astype(F32), p["k_gain"][j].astype(F32))
        if m == 0:
            x, normed = mixer_axial_gqa(*common, b, s)
        elif m == 1:
            x, normed = mixer_dilated(*common, b, s)
        elif m == 2:
            lambda_init = 0.8 - 0.6 * math.exp(-0.3 * i)
            x, normed = mixer_differential(*common, p["lam"][j], p["subln"][j], lambda_init, b, s)
        else:
            x, normed = mixer_neighbourhood(*common, p["rpb"][j], b, s)
        if i == depth - 1:
            return squared_relu_mlp(normed, x, w_up, w_down, i, split=split)
        x, normed = squared_relu_mlp(normed, x, w_up, w_down, i, next_gain=ln_mix[i + 1])


def kernel(x_prompt, x_sample, ln_mix, ln_ffn, w_up, w_down, a_w_in, a_w_out, a_q_gain, a_k_gain, b_w_in, b_w_out, b_q_gain, b_k_gain, c_w_in, c_w_out, c_q_gain, c_k_gain, c_lambda, c_subln, d_w_in, d_w_out, d_q_gain, d_k_gain, d_rpb):
    bp, s, d = x_prompt.shape
    bs = x_sample.shape[0]
    assert x_sample.shape[1:] == (s, d)
    b = bp + bs
    x = (x_prompt.reshape(bp * s, d), x_sample.reshape(bs * s, d))
    cast = lambda w: w.astype(BF16)
    mixers = [
        dict(w_in=cast(a_w_in), w_out=cast(a_w_out), q_gain=a_q_gain, k_gain=a_k_gain),
        dict(w_in=cast(b_w_in), w_out=cast(b_w_out), q_gain=b_q_gain, k_gain=b_k_gain),
        dict(w_in=cast(c_w_in), w_out=cast(c_w_out), q_gain=c_q_gain, k_gain=c_k_gain,
             lam=c_lambda, subln=c_subln),
        dict(w_in=cast(d_w_in), w_out=cast(d_w_out), q_gain=d_q_gain, k_gain=d_k_gain, rpb=d_rpb),
    ]
    y_prompt, y_sample = trunk(x, b, s, ln_mix, ln_ffn, cast(w_up), cast(w_down), mixers, split=bp * s)
    return (y_prompt.reshape(bp, s, d), y_sample.reshape(bs, s, d))
```

```python
import functools
import math

import jax
import jax.numpy as jnp
from jax import lax
from jax.experimental import pallas as pl
from jax.experimental.pallas import tpu as pltpu

F32 = jnp.float32
BF16 = jnp.bfloat16

HEAD_DIM = 128
GRID_W = 64
ROPE_THETA = 10000.0
EPS = 1e-6
LOG2E = math.log2(math.e)
Q_SCALE = HEAD_DIM ** -0.5 * LOG2E
N_MIXERS = 4
A_HEADS, A_KV_HEADS = 32, 8
B_CONFIGS = ((128, 1), (512, 4), (2048, 16))
B_HEADS, B_KV_HEADS = 16, 4
C_HEADS = 16
D_HEADS, D_KV_HEADS = 32, 8
NA_ROWS, NA_COLS = 8, 16
MASKED = -1e30
RESIDUE_LIFT_BITS = 16

LANES = 128
V7X_VMEM_BYTES = 64 * 1024 * 1024
VMEM_LIMIT_BYTES = V7X_VMEM_BYTES - 8 * 1024 * 1024
MATMUL_TEMP_BYTES = 3 * 1024 * 1024


def _tile(dim, pref, quantum):
    if dim <= pref:
        return dim
    t = (pref // quantum) * quantum
    while dim % t:
        t -= quantum
    return t


def _params(*semantics):
    return pltpu.CompilerParams(dimension_semantics=semantics, vmem_limit_bytes=VMEM_LIMIT_BYTES)


def _row_parts(x):
    return x if isinstance(x, tuple) else (x,)


def _row_part_specs(parts, tile_rows, width, row_tile, col_tile):
    specs, first = [], 0
    for p in parts:
        n = p.shape[0] // tile_rows
        specs.append(pl.BlockSpec(
            (tile_rows, width),
            lambda *g, first=first, n=n: (jnp.clip(row_tile(*g) - first, 0, n - 1), col_tile(*g))))
        first += n
    return specs


def _read_part(refs, part_tiles, row_tile, read):
    value, first = None, 0
    for ref, n in zip(refs, part_tiles):
        value = read(ref) if value is None else jnp.where(row_tile >= first, read(ref), value)
        first += n
    return value


def _rmsnorm_kernel(*refs, part_tiles):
    *x_refs, g_ref, o_ref = refs

    def emit(x_ref):
        x = x_ref[...]
        ms = jnp.mean(x * x, axis=-1, keepdims=True)
        o_ref[...] = (x * lax.rsqrt(ms + EPS) * g_ref[...]).astype(o_ref.dtype)

    first = 0
    for x_ref, n in zip(x_refs, part_tiles):
        i = pl.program_id(0)
        pl.when((i >= first) & (i < first + n))(functools.partial(emit, x_ref))
        first += n


def rmsnorm(x, gain):
    parts = _row_parts(x)
    d = parts[0].shape[1]
    t = sum(p.shape[0] for p in parts)
    tr = _tile(math.gcd(*[p.shape[0] for p in parts]), 512, 8)
    return pl.pallas_call(
        functools.partial(_rmsnorm_kernel, part_tiles=[p.shape[0] // tr for p in parts]),
        out_shape=jax.ShapeDtypeStruct((t, d), BF16),
        grid=(t // tr,),
        in_specs=_row_part_specs(parts, tr, d, lambda i: i, lambda i: 0) + [pl.BlockSpec((1, d), lambda i: (0, 0))],
        out_specs=pl.BlockSpec((tr, d), lambda i: (i, 0)),
        compiler_params=_params("parallel"),
        name="rmsnorm",
    )(*parts, gain.reshape(1, d).astype(F32))


def _lane_partial_sums(v):
    return functools.reduce(lambda a, b: a + b, [v[:, c:c + LANES] for c in range(0, v.shape[1], LANES)])


def _matmul_kernel(a_ref, w_ref, *refs, nk, act, res_tiles, first_tile, chunk, norm_in, norm_out, norm_dim):
    refs = list(refs)
    n_out = 3 if norm_out else 1
    o_ref, *norm_out_refs = refs[len(refs) - n_out:]
    refs = refs[:len(refs) - n_out]
    gain_ref = refs.pop() if norm_out else None
    sq_ref = refs.pop() if norm_in else None
    res_refs = refs
    k = pl.program_id(2)

    def residual(cols):
        return _read_part(res_refs, res_tiles, first_tile + pl.program_id(0), lambda ref: ref[:, cols])

    def emit_norm(x, cols, first):
        xg_ref, sq_out_ref = norm_out_refs
        xg_ref[:, cols] = (x * gain_ref[:, cols]).astype(xg_ref.dtype)
        sq = _lane_partial_sums(x * x)
        sq_out_ref[...] = sq if first else sq_out_ref[...] + sq

    if norm_in:
        scale = lax.rsqrt(jnp.sum(sq_ref[...], axis=-1, keepdims=True) * (1.0 / norm_dim) + EPS)

    if nk > 1:
        @pl.when(k == 0)
        def _():
            o_ref[...] = residual(slice(None)) if res_refs else jnp.zeros_like(o_ref)

    for c in range(o_ref.shape[1] // chunk):
        cols = slice(c * chunk, (c + 1) * chunk)
        part = jnp.dot(a_ref[...], w_ref[:, cols], preferred_element_type=F32)
        if nk > 1:
            part = o_ref[:, cols] + part
            o_ref[:, cols] = part
            if norm_out:
                emit_norm(part, cols, c == 0)
            continue
        if norm_in:
            part = part * scale
        if res_refs:
            part = part + residual(cols)
        if act == "relu2":
            r = jnp.maximum(part, 0.0)
            part = r * r
        o_ref[:, cols] = part.astype(o_ref.dtype)
        if norm_out:
            emit_norm(part, cols, c == 0)


def matmul(a, w, *, out_dtype, layer=None, act=None, res=None, rows=None, norm_in=None, norm_out=None,
           tm=1024, tn=1024, tk=2048, chunk=256, name="matmul"):
    row0, m = rows if rows is not None else (0, a.shape[0])
    kdim, n = w.shape[-2:]
    res_parts = _row_parts(res) if res is not None else ()
    tm = _tile(math.gcd(m, row0, *[p.shape[0] for p in res_parts[:-1]]), tm, 8)
    tn, tk = _tile(n, tn, LANES), _tile(kdim, tk, LANES)

    def window_bytes(tk):
        per_step = tm * tk * 2 + tk * tn * 2 + tm * tn * (4 * len(res_parts) + jnp.dtype(out_dtype).itemsize)
        if norm_out is not None:
            per_step += tm * tn * 2 + tm * LANES * 4
        if norm_in is not None:
            per_step += tm * norm_in[0].shape[1] * 4
        return 2 * per_step

    while window_bytes(tk) > VMEM_LIMIT_BYTES - MATMUL_TEMP_BYTES and tk % (2 * LANES) == 0:
        tk //= 2
    nk = kdim // tk
    t0 = row0 // tm
    chunk = min(chunk, tn)
    assert nk == 1 or (out_dtype == F32 and act is None), "a split K accumulates in the f32 output block"
    if w.ndim == 3:
        w_spec = pl.BlockSpec((None, tk, tn), lambda i, j, k: (layer, k, j))
    else:
        w_spec = pl.BlockSpec((tk, tn), lambda i, j, k: (k, j))
    in_specs = [pl.BlockSpec((tm, tk), lambda i, j, k: (t0 + i, k)), w_spec]
    args = [a, w]
    in_specs += _row_part_specs(res_parts, tm, tn, lambda i, j, k: t0 + i, lambda i, j, k: j)
    args += res_parts
    norm_dim = None
    if norm_in is not None:
        sq, norm_dim = norm_in
        assert nk == 1, "the row scale is applied where the whole product is formed"
        in_specs.append(pl.BlockSpec((tm, sq.shape[1]), lambda i, j, k: (t0 + i, 0)))
        args.append(sq)
    out_shape = jax.ShapeDtypeStruct((m, n), out_dtype)
    out_specs = pl.BlockSpec((tm, tn), lambda i, j, k: (i, j))
    if norm_out is not None:
        assert out_dtype == F32
        in_specs.append(pl.BlockSpec((1, tn), lambda i, j, k: (0, j)))
        args.append(norm_out.reshape(1, n).astype(F32))
        out_shape = (out_shape, jax.ShapeDtypeStruct((m, n), BF16),
                     jax.ShapeDtypeStruct((m, n // tn * LANES), F32))
        out_specs = (out_specs, pl.BlockSpec((tm, tn), lambda i, j, k: (i, j)),
                     pl.BlockSpec((tm, LANES), lambda i, j, k: (i, j)))
    return pl.pallas_call(
        functools.partial(_matmul_kernel, nk=nk, act=act, chunk=chunk, first_tile=t0,
                          res_tiles=[p.shape[0] // tm for p in res_parts],
                          norm_in=norm_in is not None, norm_out=norm_out is not None, norm_dim=norm_dim),
        out_shape=out_shape,
        grid=(m // tm, n // tn, nk),
        in_specs=in_specs,
        out_specs=out_specs,
        compiler_params=_params("parallel", "parallel", "arbitrary"),
        name=name,
    )(*args)


def rope_tables(pos, dim):
    inv = ROPE_THETA ** (-jnp.arange(0, dim, 2, dtype=F32) / dim)
    ang = pos.astype(F32)[:, None] * inv[None, :]
    ang = jnp.concatenate([ang, ang], axis=-1)
    return jnp.cos(ang), jnp.sin(ang)


def full_rope_tables(s):
    cos, sin = rope_tables(jnp.arange(s), HEAD_DIM)
    lane = jnp.arange(HEAD_DIM)
    sin_signed = jnp.where(lane < HEAD_DIM // 2, -sin, sin)
    return ((HEAD_DIM // 2,), (cos, sin_signed))


def axial_rope_tables(s):
    half = HEAD_DIM // 2
    t = jnp.arange(s)
    cr, sr = rope_tables(t // GRID_W, half)
    cc, sc = rope_tables(t % GRID_W, half)
    cos = jnp.concatenate([cr, cc], axis=-1)
    sin = jnp.concatenate([sr, sc], axis=-1)
    low = (jnp.arange(HEAD_DIM) % half) < half // 2
    return ((HEAD_DIM - half // 2, half // 2),
            (cos, jnp.where(low, -sin, 0.0), jnp.where(low, 0.0, sin)))


def _prep_kernel(x_ref, gain_ref, *refs, shifts):
    *tab_refs, o_ref = refs
    tables = [t[...] for t in tab_refs]
    averager = jnp.full((HEAD_DIM, HEAD_DIM), 1.0 / HEAD_DIM, BF16)
    for h0 in range(0, x_ref.shape[1], HEAD_DIM):
        head = slice(h0, h0 + HEAD_DIM)
        x = x_ref[:, head].astype(F32)
        ms = jnp.dot((x * x).astype(BF16), averager, preferred_element_type=F32)
        y = x * lax.rsqrt(ms + EPS) * gain_ref[:, head]
        if shifts:
            r = y * tables[0]
            for sh, t in zip(shifts, tables[1:]):
                r = r + pltpu.roll(y, sh, 1) * t
            y = r
        o_ref[:, head] = y.astype(o_ref.dtype)


def in_projection(normed, w_in, layer, segments, rope, seq, tiles_per_group, v_tile, *, tn, name):
    gains = jnp.concatenate([jnp.tile(jnp.ones((HEAD_DIM,), F32) if g is None else g.astype(F32), n)
                             for g, n in segments])[None]
    shifts, tables = rope if rope is not None else ((), ())
    a, norm_in = normed
    proj = matmul(a, w_in, layer=layer, out_dtype=BF16, norm_in=norm_in, tk=a.shape[1], name=name)
    t, n = proj.shape
    ts = _tile(seq, 1024, 16)
    per_seq = seq // ts
    qk_tiles = n // tn // tiles_per_group * v_tile
    tile = lambda j: j // v_tile * tiles_per_group + j % v_tile
    return pl.pallas_call(
        functools.partial(_prep_kernel, shifts=shifts),
        out_shape=jax.ShapeDtypeStruct((t, n), BF16),
        grid=(t // ts, qk_tiles),
        in_specs=[pl.BlockSpec((ts, tn), lambda i, j: (i, tile(j))), pl.BlockSpec((1, tn), lambda i, j: (0, tile(j)))]
        + [pl.BlockSpec((ts, HEAD_DIM), lambda i, j: (i % per_seq, 0))] * len(tables),
        out_specs=pl.BlockSpec((ts, tn), lambda i, j: (i, tile(j))),
        input_output_aliases={0: 0},
        compiler_params=_params("parallel", "parallel"),
        name=name + "_prep",
    )(proj, gains, *[tab.astype(F32) for tab in tables])


def _stack_heads(x, n):
    return jnp.concatenate([x[:, h * HEAD_DIM:(h + 1) * HEAD_DIM] for h in range(n)], axis=0)


def _unstack_heads(x, n):
    rows = x.shape[0] // n
    return jnp.concatenate([x[h * rows:(h + 1) * rows] for h in range(n)], axis=1)


def _qk(q, k):
    return lax.dot_general(q, k, (((1,), (1,)), ((), ())), preferred_element_type=F32)


def _dense_gqa_kernel(q_ref, k_ref, v_ref, o_ref, *, group, chains):
    k, v = k_ref[0], v_ref[0]
    per = group // chains

    def scores(c):
        return _qk(_stack_heads(q_ref[0, :, c * per * HEAD_DIM:(c + 1) * per * HEAD_DIM], per), k)

    ahead = [scores(c) for c in range(min(2, chains))]
    for c in range(chains):
        cols = slice(c * per * HEAD_DIM, (c + 1) * per * HEAD_DIM)
        s = ahead.pop(0)
        if c + 2 < chains:
            ahead.append(scores(c + 2))
        m = jnp.max(s, axis=-1, keepdims=True)
        p = jnp.exp2(s - m)
        l = jnp.sum(p, axis=-1, keepdims=True)
        o = jnp.dot(p.astype(BF16), v, preferred_element_type=F32) / l
        o_ref[0, :, cols] = _unstack_heads(o, per).astype(o_ref.dtype)


def dense_gqa(proj, *, kv_heads, group, tq=256, chains=4):
    b, s, _ = proj.shape
    tq = _tile(s, tq, 16)
    w = group * HEAD_DIM
    k_block0 = kv_heads * group
    v_block0 = k_block0 + kv_heads
    return pl.pallas_call(
        functools.partial(_dense_gqa_kernel, group=group, chains=chains),
        out_shape=jax.ShapeDtypeStruct((b, s, kv_heads * w), BF16),
        grid=(b, kv_heads, s // tq),
        in_specs=[
            pl.BlockSpec((1, tq, w), lambda bi, g, i: (bi, i, g)),
            pl.BlockSpec((1, s, HEAD_DIM), lambda bi, g, i: (bi, 0, k_block0 + g)),
            pl.BlockSpec((1, s, HEAD_DIM), lambda bi, g, i: (bi, 0, v_block0 + g)),
        ],
        out_specs=pl.BlockSpec((1, tq, w), lambda bi, g, i: (bi, i, g)),
        compiler_params=_params("parallel", "parallel", "arbitrary"),
        name="dense_gqa",
    )(proj, proj, proj)


def _dilated_kernel(*refs, configs, seq, group):
    o_ref = refs[-1]
    i = pl.program_id(2)
    tq = o_ref.shape[1]

    chains = []
    for gi, (window, dil) in enumerate(configs):
        reach = (window // (2 * dil)) * dil
        pad = -(-reach // tq) * tq
        wlen = min(seq, tq + 2 * pad)
        start = pl.multiple_of(jnp.clip(i * tq - pad, 0, seq - wlen), tq)
        u = (start + lax.broadcasted_iota(jnp.int32, (tq, wlen), 1)
             - i * tq - lax.broadcasted_iota(jnp.int32, (tq, wlen), 0) + reach)
        if dil > 1:
            u = u + ((u & (dil - 1)) << RESIDUE_LIFT_BITS)
        ok = lax.bitcast_convert_type(u, jnp.uint32) <= jnp.uint32(2 * reach)
        bias = jnp.where(ok, 0.0, MASKED)
        per = group if wlen <= 4 * tq else max(1, group // 2)
        for h0 in range(0, group, per):
            chains.append((gi, h0, per, start, wlen, bias))

    def scores(c):
        gi, h0, per, start, wlen, _ = chains[c]
        q_ref, k_ref = refs[3 * gi], refs[3 * gi + 1]
        q = _stack_heads(q_ref[0, :, h0 * HEAD_DIM:(h0 + per) * HEAD_DIM], per)
        return _qk(q, k_ref[0, pl.ds(start, wlen), :])

    outs = [[None] * group for _ in configs]
    lses = [[None] * group for _ in configs]
    ahead = [scores(c) for c in range(min(2, len(chains)))]
    for c, (gi, h0, per, start, wlen, bias) in enumerate(chains):
        s = ahead.pop(0)
        if c + 2 < len(chains):
            ahead.append(scores(c + 2))
        s = (s.reshape(per, tq, wlen) + bias[None]).reshape(per * tq, wlen)
        m = jnp.max(s, axis=-1, keepdims=True)
        p = jnp.exp2(s - m)
        l = jnp.sum(p, axis=-1, keepdims=True)
        v_ref = refs[3 * gi + 2]
        o = jnp.dot(p.astype(BF16), v_ref[0, pl.ds(start, wlen), :], preferred_element_type=F32) / l
        lse = m + jnp.log2(l)
        for h in range(per):
            outs[gi][h0 + h] = o[h * tq:(h + 1) * tq]
            lses[gi][h0 + h] = lse[h * tq:(h + 1) * tq]

    for h in range(group):
        lse_h = [lses[gi][h] for gi in range(len(configs))]
        top = functools.reduce(jnp.maximum, lse_h)
        ws = [jnp.exp2(x - top) for x in lse_h]
        o = sum(w * outs[gi][h] for gi, w in enumerate(ws)) / sum(ws)
        o_ref[0, :, h * HEAD_DIM:(h + 1) * HEAD_DIM] = o.astype(o_ref.dtype)


def dilated_attention(proj, *, configs=B_CONFIGS, tq=256):
    b, s, _ = proj.shape
    tq = _tile(s, tq, LANES)
    for window, dil in configs:
        assert dil & (dil - 1) == 0, "the residue test uses a power-of-two dilation"
        assert 2 * window + s < 1 << RESIDUE_LIFT_BITS, "a lifted residue must clear every in-range offset"
    group = B_HEADS // B_KV_HEADS
    w = group * HEAD_DIM
    per_group_blocks = B_HEADS + 2 * B_KV_HEADS
    in_specs, args = [], []
    for gi in range(len(configs)):
        q0 = gi * per_group_blocks // group
        k0 = gi * per_group_blocks + B_HEADS
        in_specs += [
            pl.BlockSpec((1, tq, w), lambda bi, h, i, q0=q0: (bi, i, q0 + h)),
            pl.BlockSpec((1, s, HEAD_DIM), lambda bi, h, i, k0=k0: (bi, 0, k0 + h)),
            pl.BlockSpec((1, s, HEAD_DIM), lambda bi, h, i, k0=k0: (bi, 0, k0 + B_KV_HEADS + h)),
        ]
        args += [proj, proj, proj]
    return pl.pallas_call(
        functools.partial(_dilated_kernel, configs=configs, seq=s, group=group),
        out_shape=jax.ShapeDtypeStruct((b, s, B_HEADS * HEAD_DIM), BF16),
        grid=(b, B_KV_HEADS, s // tq),
        in_specs=in_specs,
        out_specs=pl.BlockSpec((1, tq, w), lambda bi, h, i: (bi, i, h)),
        compiler_params=_params("parallel", "parallel", "arbitrary"),
        name="dilated_attention",
    )(*args)


def _differential_kernel(q_ref, k_ref, v_ref, lam_ref, sub_ref, o_ref, *, lambda_init, chains):
    lp = lam_ref[...]
    lam = (jnp.exp(jnp.sum(lp[0:1] * lp[1:2], axis=-1, keepdims=True))
           - jnp.exp(jnp.sum(lp[2:3] * lp[3:4], axis=-1, keepdims=True)) + lambda_init)
    k, v = k_ref[0], v_ref[0]
    rows = q_ref.shape[1] // chains
    def scores(c):
        q = q_ref[0, c * rows:(c + 1) * rows, :]
        return [_qk(q[:, mi * HEAD_DIM:(mi + 1) * HEAD_DIM], k[:, mi * HEAD_DIM:(mi + 1) * HEAD_DIM])
                for mi in range(2)]

    nxt = scores(0)
    for c in range(chains):
        cur = nxt
        if c + 1 < chains:
            nxt = scores(c + 1)
        probs = []
        for s in cur:
            m = jnp.max(s, axis=-1, keepdims=True)
            p = jnp.exp2(s - m)
            probs.append((p, jnp.sum(p, axis=-1, keepdims=True)))
        (p0, l0), (p1, l1) = probs
        a = p0 - p1 * (lam * l0 / l1)
        o = jnp.dot(a.astype(BF16), v, preferred_element_type=F32) / l0
        ms = jnp.mean(o * o, axis=-1, keepdims=True)
        o = o * lax.rsqrt(ms + EPS) * sub_ref[...] * (1.0 - lambda_init)
        o_ref[0, c * rows:(c + 1) * rows, :] = o.astype(o_ref.dtype)


def differential_attention(proj, lam_params, subln, lambda_init, *, tq=1024, chains=4):
    b, s, _ = proj.shape
    tq = _tile(s, tq, 16)
    w = 2 * HEAD_DIM
    return pl.pallas_call(
        functools.partial(_differential_kernel, lambda_init=lambda_init, chains=chains),
        out_shape=jax.ShapeDtypeStruct((b, s, C_HEADS * w), BF16),
        grid=(b, C_HEADS, s // tq),
        in_specs=[
            pl.BlockSpec((1, tq, w), lambda bi, h, i: (bi, i, h)),
            pl.BlockSpec((1, s, w), lambda bi, h, i: (bi, 0, C_HEADS + h)),
            pl.BlockSpec((1, s, w), lambda bi, h, i: (bi, 0, 2 * C_HEADS + h)),
            pl.BlockSpec((4, HEAD_DIM), lambda bi, h, i: (0, 0)),
            pl.BlockSpec((1, w), lambda bi, h, i: (0, 0)),
        ],
        out_specs=pl.BlockSpec((1, tq, w), lambda bi, h, i: (bi, i, h)),
        compiler_params=_params("parallel", "parallel", "arbitrary"),
        name="differential_attention",
    )(proj, proj, proj, lam_params.astype(F32), subln.reshape(1, w).astype(F32))


def _na_bias_kernel(rpb_ref, o_ref):
    rows = lax.broadcasted_iota(jnp.int32, (GRID_W, LANES), 0)
    lane = lax.broadcasted_iota(jnp.int32, (GRID_W, LANES), 1)
    n_rel = 2 * NA_ROWS - 1
    toeplitz = []
    for rho in range(n_rel):
        z = jnp.broadcast_to(rpb_ref[0, rho:rho + 1, :], (GRID_W, LANES))
        z = pltpu.roll(z, LANES - (NA_COLS - 1), 1)
        bit = 1
        while bit < GRID_W:
            z = jnp.where((rows & bit) != 0, pltpu.roll(z, bit, 1), z)
            bit *= 2
        toeplitz.append(z)
    kc = lane & (GRID_W - 1)
    first = jnp.clip(rows - NA_COLS // 2, 0, GRID_W - NA_COLS)
    inside = (kc >= first) & (kc < first + NA_COLS)
    for d in range(NA_ROWS):
        for pair in range(NA_ROWS // 2):
            even = toeplitz[d + 2 * pair]
            odd = pltpu.roll(toeplitz[d + 2 * pair + 1], GRID_W, 1)
            tile = jnp.where(lane < GRID_W, even, odd)
            o_ref[d, 0, :, pair * LANES:(pair + 1) * LANES] = jnp.where(inside, tile * LOG2E, MASKED)


def na_bias_table(rpb):
    h = rpb.shape[0]
    padded = jnp.zeros((h, 2 * NA_ROWS, LANES), F32).at[:, :2 * NA_ROWS - 1, :2 * NA_COLS - 1].set(rpb.astype(F32))
    return pl.pallas_call(
        _na_bias_kernel,
        out_shape=jax.ShapeDtypeStruct((NA_ROWS, h, GRID_W, NA_ROWS * GRID_W), F32),
        grid=(h,),
        in_specs=[pl.BlockSpec((1, 2 * NA_ROWS, LANES), lambda i: (i, 0, 0))],
        out_specs=pl.BlockSpec((NA_ROWS, 1, GRID_W, NA_ROWS * GRID_W), lambda i: (0, i, 0, 0)),
        compiler_params=_params("parallel"),
        name="na_bias_table",
    )(padded)


def _neighbourhood_kernel(q_ref, k_ref, v_ref, b_ref, o_ref, *, n_rows, group):
    win = NA_ROWS * GRID_W

    def row(r, carry):
        first = jnp.clip(r - NA_ROWS // 2, 0, n_rows - NA_ROWS)
        q0 = pl.multiple_of(r * GRID_W, GRID_W)
        k0 = pl.multiple_of(first * GRID_W, GRID_W)
        q = _stack_heads(q_ref[0, pl.ds(q0, GRID_W), :], group)
        s = _qk(q, k_ref[0, pl.ds(k0, win), :])
        s = (s.reshape(group, GRID_W, win) + b_ref[first - r + NA_ROWS - 1]).reshape(group * GRID_W, win)
        m = jnp.max(s, axis=-1, keepdims=True)
        p = jnp.exp2(s - m)
        l = jnp.sum(p, axis=-1, keepdims=True)
        o = jnp.dot(p.astype(BF16), v_ref[0, pl.ds(k0, win), :], preferred_element_type=F32) / l
        o_ref[0, pl.ds(q0, GRID_W), :] = _unstack_heads(o, group).astype(o_ref.dtype)
        return carry

    lax.fori_loop(0, n_rows, row, 0, unroll=16)


def neighbourhood_attention(proj, bias, *, kv_heads, group):
    b, s, _ = proj.shape
    n_rows = s // GRID_W
    assert n_rows >= NA_ROWS, "the key window is NA_ROWS grid rows tall"
    w = group * HEAD_DIM
    k_block0 = kv_heads * group
    v_block0 = k_block0 + kv_heads
    return pl.pallas_call(
        functools.partial(_neighbourhood_kernel, n_rows=n_rows, group=group),
        out_shape=jax.ShapeDtypeStruct((b, s, kv_heads * w), BF16),
        grid=(b, kv_heads),
        in_specs=[
            pl.BlockSpec((1, s, w), lambda bi, g: (bi, 0, g)),
            pl.BlockSpec((1, s, HEAD_DIM), lambda bi, g: (bi, 0, k_block0 + g)),
            pl.BlockSpec((1, s, HEAD_DIM), lambda bi, g: (bi, 0, v_block0 + g)),
            pl.BlockSpec((NA_ROWS, group, GRID_W, NA_ROWS * GRID_W), lambda bi, g: (0, g, 0, 0)),
        ],
        out_specs=pl.BlockSpec((1, s, w), lambda bi, g: (bi, 0, g)),
        compiler_params=_params("parallel", "arbitrary"),
        name="neighbourhood_attention",
    )(proj, proj, proj, bias)


IN_TILE = 1024


def _with_next_norm(outputs):
    x, xg, sq = outputs
    return x, (xg, (sq, x.shape[1]))


def _out_projection(o, x, w_out, layer, next_gain, name):
    tn = 1024 // len(_row_parts(x))
    return _with_next_norm(matmul(o.reshape(-1, o.shape[-1]), w_out, layer=layer, out_dtype=F32, res=x,
                                  norm_out=next_gain, tk=o.shape[-1], tn=tn, name=name))


def mixer_axial_gqa(normed, x, w_in, w_out, layer, next_gain, q_gain, k_gain, b, s):
    segments = [(q_gain * Q_SCALE, A_HEADS), (k_gain, A_KV_HEADS), (None, A_KV_HEADS)]
    tiles = (A_HEADS + 2 * A_KV_HEADS) * HEAD_DIM // IN_TILE
    proj = in_projection(normed, w_in, layer, segments, axial_rope_tables(s), s, tiles,
                         (A_HEADS + A_KV_HEADS) * HEAD_DIM // IN_TILE, tn=IN_TILE, name="a_in")
    o = dense_gqa(proj.reshape(b, s, -1), kv_heads=A_KV_HEADS, group=A_HEADS // A_KV_HEADS)
    return _out_projection(o, x, w_out, layer, next_gain, "a_out")


def mixer_dilated(normed, x, w_in, w_out, layer, next_gain, q_gain, k_gain, b, s):
    segments = []
    for gi in range(len(B_CONFIGS)):
        segments += [(q_gain[gi] * Q_SCALE, B_HEADS), (k_gain[gi], B_KV_HEADS), (None, B_KV_HEADS)]
    tn = B_KV_HEADS * HEAD_DIM
    tiles = (B_HEADS + 2 * B_KV_HEADS) * HEAD_DIM // tn
    proj = in_projection(normed, w_in, layer, segments, full_rope_tables(s), s, tiles, tiles - 1, tn=tn, name="b_in")
    o = dilated_attention(proj.reshape(b, s, -1))
    return _out_projection(o, x, w_out, layer, next_gain, "b_out")


def mixer_differential(normed, x, w_in, w_out, layer, next_gain, q_gain, k_gain, lam_params, subln, lambda_init,
                       b, s):
    segments = [(q_gain * Q_SCALE, 2 * C_HEADS), (k_gain, 2 * C_HEADS), (None, 2 * C_HEADS)]
    tiles = 3 * 2 * C_HEADS * HEAD_DIM // IN_TILE
    proj = in_projection(normed, w_in, layer, segments, full_rope_tables(s), s, tiles, 2 * tiles // 3,
                         tn=IN_TILE, name="c_in")
    o = differential_attention(proj.reshape(b, s, -1), lam_params, subln, lambda_init)
    return _out_projection(o, x, w_out, layer, next_gain, "c_out")


def mixer_neighbourhood(normed, x, w_in, w_out, layer, next_gain, q_gain, k_gain, rpb, b, s):
    segments = [(q_gain * Q_SCALE, D_HEADS), (k_gain, D_KV_HEADS), (None, D_KV_HEADS)]
    tiles = (D_HEADS + 2 * D_KV_HEADS) * HEAD_DIM // IN_TILE
    proj = in_projection(normed, w_in, layer, segments, None, s, tiles,
                         (D_HEADS + D_KV_HEADS) * HEAD_DIM // IN_TILE, tn=IN_TILE, name="d_in")
    o = neighbourhood_attention(proj.reshape(b, s, -1), na_bias_table(rpb),
                                kv_heads=D_KV_HEADS, group=D_HEADS // D_KV_HEADS)
    return _out_projection(o, x, w_out, layer, next_gain, "d_out")


def squared_relu_mlp(normed, x, w_up, w_down, layer, next_gain=None, split=None):
    a, norm_in = normed
    u = matmul(a, w_up, layer=layer, out_dtype=BF16, act="relu2", norm_in=norm_in, tk=a.shape[1], name="mlp_up")
    down = functools.partial(matmul, u, w_down, layer=layer, out_dtype=F32, res=x, tk=4096, name="mlp_down")
    if split is None:
        return _with_next_norm(down(norm_out=next_gain))
    return down(rows=(0, split)), down(rows=(split, u.shape[0] - split))


def trunk(x, b, s, ln_mix, ln_ffn, w_up, w_down, mixers, split):
    depth = ln_mix.shape[0]
    normed = (rmsnorm(x, ln_mix[0]), None)
    for i in range(depth):
        m, j = i % N_MIXERS, i // N_MIXERS
        p = mixers[m]
        common = (normed, x, p["w_in"], p["w_out"], j, ln_ffn[i],
                  p["q_gain"][j].astype(F32), p["k_gain"][j].astype(F32))
        if m == 0:
            x, normed = mixer_axial_gqa(*common, b, s)
        elif m == 1:
            x, normed = mixer_dilated(*common, b, s)
        elif m == 2:
            lambda_init = 0.8 - 0.6 * math.exp(-0.3 * i)
            x, normed = mixer_differential(*common, p["lam"][j], p["subln"][j], lambda_init, b, s)
        else:
            x, normed = mixer_neighbourhood(*common, p["rpb"][j], b, s)
        if i == depth - 1:
            return squared_relu_mlp(normed, x, w_up, w_down, i, split=split)
        x, normed = squared_relu_mlp(normed, x, w_up, w_down, i, next_gain=ln_mix[i + 1])


def kernel(x_prompt, x_sample, ln_mix, ln_ffn, w_up, w_down, a_w_in, a_w_out, a_q_gain, a_k_gain, b_w_in, b_w_out, b_q_gain, b_k_gain, c_w_in, c_w_out, c_q_gain, c_k_gain, c_lambda, c_subln, d_w_in, d_w_out, d_q_gain, d_k_gain, d_rpb):
    bp, s, d = x_prompt.shape
    bs = x_sample.shape[0]
    assert x_sample.shape[1:] == (s, d)
    b = bp + bs
    x = (x_prompt.reshape(bp * s, d), x_sample.reshape(bs * s, d))
    cast = lambda w: w.astype(BF16)
    mixers = [
        dict(w_in=cast(a_w_in), w_out=cast(a_w_out), q_gain=a_q_gain, k_gain=a_k_gain),
        dict(w_in=cast(b_w_in), w_out=cast(b_w_out), q_gain=b_q_gain, k_gain=b_k_gain),
        dict(w_in=cast(c_w_in), w_out=cast(c_w_out), q_gain=c_q_gain, k_gain=c_k_gain,
             lam=c_lambda, subln=c_subln),
        dict(w_in=cast(d_w_in), w_out=cast(d_w_out), q_gain=d_q_gain, k_gain=d_k_gain, rpb=d_rpb),
    ]
    y_prompt, y_sample = trunk(x, b, s, ln_mix, ln_ffn, cast(w_up), cast(w_down), mixers, split=bp * s)
    return (y_prompt.reshape(bp, s, d), y_sample.reshape(bs, s, d))
```

```python
import functools
import math

import jax
import jax.numpy as jnp
from jax import lax
from jax.experimental import pallas as pl
from jax.experimental.pallas import tpu as pltpu

F32 = jnp.float32
BF16 = jnp.bfloat16

HEAD_DIM = 128
GRID_W = 64
ROPE_THETA = 10000.0
EPS = 1e-6
LOG2E = math.log2(math.e)
Q_SCALE = HEAD_DIM ** -0.5 * LOG2E
N_MIXERS = 4
A_HEADS, A_KV_HEADS = 32, 8
B_CONFIGS = ((128, 1), (512, 4), (2048, 16))
B_HEADS, B_KV_HEADS = 16, 4
C_HEADS = 16
D_HEADS, D_KV_HEADS = 32, 8
NA_ROWS, NA_COLS = 8, 16
MASKED = -1e30
RESIDUE_LIFT_BITS = 16

LANES = 128
V7X_VMEM_BYTES = 64 * 1024 * 1024
VMEM_LIMIT_BYTES = V7X_VMEM_BYTES - 8 * 1024 * 1024
MATMUL_TEMP_BYTES = 3 * 1024 * 1024
CAST_ROWS, CAST_COLS = 256, 1024


def _tile(dim, pref, quantum):
    if dim <= pref:
        return dim
    t = (pref // quantum) * quantum
    while dim % t:
        t -= quantum
    return t


def _params(*semantics):
    return pltpu.CompilerParams(dimension_semantics=semantics, vmem_limit_bytes=VMEM_LIMIT_BYTES)


def _row_parts(x):
    return x if isinstance(x, tuple) else (x,)


def _row_part_specs(parts, tile_rows, width, row_tile, col_tile):
    specs, first = [], 0
    for p in parts:
        n = p.shape[0] // tile_rows
        specs.append(pl.BlockSpec(
            (tile_rows, width),
            lambda *g, first=first, n=n: (jnp.clip(row_tile(*g) - first, 0, n - 1), col_tile(*g))))
        first += n
    return specs


def _read_part(refs, part_tiles, row_tile, read):
    value, first = None, 0
    for ref, n in zip(refs, part_tiles):
        value = read(ref) if value is None else jnp.where(row_tile >= first, read(ref), value)
        first += n
    return value


def _rmsnorm_kernel(*refs, part_tiles):
    *x_refs, g_ref, o_ref = refs

    def emit(x_ref):
        x = x_ref[...]
        ms = jnp.mean(x * x, axis=-1, keepdims=True)
        o_ref[...] = (x * lax.rsqrt(ms + EPS) * g_ref[...]).astype(o_ref.dtype)

    first = 0
    for x_ref, n in zip(x_refs, part_tiles):
        i = pl.program_id(0)
        pl.when((i >= first) & (i < first + n))(functools.partial(emit, x_ref))
        first += n


def rmsnorm(x, gain):
    parts = _row_parts(x)
    d = parts[0].shape[1]
    t = sum(p.shape[0] for p in parts)
    tr = _tile(math.gcd(*[p.shape[0] for p in parts]), 512, 8)
    return pl.pallas_call(
        functools.partial(_rmsnorm_kernel, part_tiles=[p.shape[0] // tr for p in parts]),
        out_shape=jax.ShapeDtypeStruct((t, d), BF16),
        grid=(t // tr,),
        in_specs=_row_part_specs(parts, tr, d, lambda i: i, lambda i: 0) + [pl.BlockSpec((1, d), lambda i: (0, 0))],
        out_specs=pl.BlockSpec((tr, d), lambda i: (i, 0)),
        compiler_params=_params("parallel"),
        name="rmsnorm",
    )(*parts, gain.reshape(1, d).astype(F32))


def _lane_partial_sums(v):
    return functools.reduce(lambda a, b: a + b, [v[:, c:c + LANES] for c in range(0, v.shape[1], LANES)])


def _matmul_kernel(a_ref, w_ref, *refs, nk, act, res_tiles, first_tile, chunk, norm_in, norm_out, norm_dim,
                   n_casts):
    refs = list(refs)
    n_out = (3 if norm_out else 1) + n_casts
    outs, refs = refs[len(refs) - n_out:], refs[:len(refs) - n_out]
    o_ref, norm_out_refs, cast_out_refs = outs[0], outs[1:n_out - n_casts], outs[n_out - n_casts:]
    cast_in_refs = [refs.pop() for _ in range(n_casts)][::-1]
    gain_ref = refs.pop() if norm_out else None
    sq_ref = refs.pop() if norm_in else None
    res_refs = refs
    k = pl.program_id(2)

    for src_ref, dst_ref in zip(cast_in_refs, cast_out_refs):
        dst_ref[...] = src_ref[...].astype(dst_ref.dtype)

    def residual(cols):
        return _read_part(res_refs, res_tiles, first_tile + pl.program_id(0), lambda ref: ref[:, cols])

    def emit_norm(x, cols, first):
        xg_ref, sq_out_ref = norm_out_refs
        xg_ref[:, cols] = (x * gain_ref[:, cols]).astype(xg_ref.dtype)
        sq = _lane_partial_sums(x * x)
        sq_out_ref[...] = sq if first else sq_out_ref[...] + sq

    if norm_in:
        scale = lax.rsqrt(jnp.sum(sq_ref[...], axis=-1, keepdims=True) * (1.0 / norm_dim) + EPS)

    if nk > 1:
        @pl.when(k == 0)
        def _():
            o_ref[...] = residual(slice(None)) if res_refs else jnp.zeros_like(o_ref)

    for c in range(o_ref.shape[1] // chunk):
        cols = slice(c * chunk, (c + 1) * chunk)
        part = jnp.dot(a_ref[...], w_ref[:, cols], preferred_element_type=F32)
        if nk > 1:
            part = o_ref[:, cols] + part
            o_ref[:, cols] = part
            if norm_out:
                emit_norm(part, cols, c == 0)
            continue
        if norm_in:
            part = part * scale
        if res_refs:
            part = part + residual(cols)
        if act == "relu2":
            r = jnp.maximum(part, 0.0)
            part = r * r
        o_ref[:, cols] = part.astype(o_ref.dtype)
        if norm_out:
            emit_norm(part, cols, c == 0)


def matmul(a, w, *, out_dtype, layer=None, act=None, res=None, rows=None, norm_in=None, norm_out=None,
           casts=(), tm=1024, tn=1024, tk=2048, chunk=256, name="matmul"):
    row0, m = rows if rows is not None else (0, a.shape[0])
    kdim, n = w.shape[-2:]
    res_parts = _row_parts(res) if res is not None else ()
    tm = _tile(math.gcd(m, row0, *[p.shape[0] for p in res_parts[:-1]]), tm, 8)
    tn, tk = _tile(n, tn, LANES), _tile(kdim, tk, LANES)

    def window_bytes(tk):
        per_step = tm * tk * 2 + tk * tn * 2 + tm * tn * (4 * len(res_parts) + jnp.dtype(out_dtype).itemsize)
        if norm_out is not None:
            per_step += tm * tn * 2 + tm * LANES * 4
        if norm_in is not None:
            per_step += tm * norm_in[0].shape[1] * 4
        per_step += len(casts) * CAST_ROWS * CAST_COLS * (4 + 2)
        return 2 * per_step

    while window_bytes(tk) > VMEM_LIMIT_BYTES - MATMUL_TEMP_BYTES and tk % (2 * LANES) == 0:
        tk //= 2
    nk = kdim // tk
    t0 = row0 // tm
    chunk = min(chunk, tn)
    assert nk == 1 or (out_dtype == F32 and act is None), "a split K accumulates in the f32 output block"
    if w.ndim == 3:
        w_spec = pl.BlockSpec((None, tk, tn), lambda i, j, k: (layer, k, j))
    else:
        w_spec = pl.BlockSpec((tk, tn), lambda i, j, k: (k, j))
    in_specs = [pl.BlockSpec((tm, tk), lambda i, j, k: (t0 + i, k)), w_spec]
    args = [a, w]
    in_specs += _row_part_specs(res_parts, tm, tn, lambda i, j, k: t0 + i, lambda i, j, k: j)
    args += res_parts
    norm_dim = None
    if norm_in is not None:
        sq, norm_dim = norm_in
        assert nk == 1, "the row scale is applied where the whole product is formed"
        in_specs.append(pl.BlockSpec((tm, sq.shape[1]), lambda i, j, k: (t0 + i, 0)))
        args.append(sq)
    out_shape = jax.ShapeDtypeStruct((m, n), out_dtype)
    out_specs = pl.BlockSpec((tm, tn), lambda i, j, k: (i, j))
    if norm_out is not None:
        assert out_dtype == F32
        in_specs.append(pl.BlockSpec((1, tn), lambda i, j, k: (0, j)))
        args.append(norm_out.reshape(1, n).astype(F32))
        out_shape = (out_shape, jax.ShapeDtypeStruct((m, n), BF16),
                     jax.ShapeDtypeStruct((m, n // tn * LANES), F32))
        out_specs = (out_specs, pl.BlockSpec((tm, tn), lambda i, j, k: (i, j)),
                     pl.BlockSpec((tm, LANES), lambda i, j, k: (i, j)))
    grid = (m // tm, n // tn, nk)
    for src, src_layer in casts:
        rows_, cols_ = src.shape[-2] // CAST_ROWS, src.shape[-1] // CAST_COLS
        assert rows_ * cols_ <= grid[0] * grid[1] * grid[2], "not enough grid steps to cast every block"

        def block(i, j, k, cols_=cols_, last=rows_ * cols_ - 1):
            step = jnp.minimum((i * grid[1] + j) * grid[2] + k, last)
            return step // cols_, step % cols_

        in_specs.append(pl.BlockSpec((None, CAST_ROWS, CAST_COLS),
                                     lambda i, j, k, block=block, src_layer=src_layer: (src_layer, *block(i, j, k))))
        args.append(src)
        out_shape = (*_row_parts(out_shape), jax.ShapeDtypeStruct(src.shape[-2:], BF16))
        out_specs = (*_row_parts(out_specs), pl.BlockSpec((CAST_ROWS, CAST_COLS), block))
    return pl.pallas_call(
        functools.partial(_matmul_kernel, nk=nk, act=act, chunk=chunk, first_tile=t0,
                          res_tiles=[p.shape[0] // tm for p in res_parts], n_casts=len(casts),
                          norm_in=norm_in is not None, norm_out=norm_out is not None, norm_dim=norm_dim),
        out_shape=out_shape,
        grid=grid,
        in_specs=in_specs,
        out_specs=out_specs,
        compiler_params=_params("parallel", "parallel", "arbitrary"),
        name=name,
    )(*args)


def rope_tables(pos, dim):
    inv = ROPE_THETA ** (-jnp.arange(0, dim, 2, dtype=F32) / dim)
    ang = pos.astype(F32)[:, None] * inv[None, :]
    ang = jnp.concatenate([ang, ang], axis=-1)
    return jnp.cos(ang), jnp.sin(ang)


def full_rope_tables(s):
    cos, sin = rope_tables(jnp.arange(s), HEAD_DIM)
    lane = jnp.arange(HEAD_DIM)
    sin_signed = jnp.where(lane < HEAD_DIM // 2, -sin, sin)
    return ((HEAD_DIM // 2,), (cos, sin_signed))


def axial_rope_tables(s):
    half = HEAD_DIM // 2
    t = jnp.arange(s)
    cr, sr = rope_tables(t // GRID_W, half)
    cc, sc = rope_tables(t % GRID_W, half)
    cos = jnp.concatenate([cr, cc], axis=-1)
    sin = jnp.concatenate([sr, sc], axis=-1)
    low = (jnp.arange(HEAD_DIM) % half) < half // 2
    return ((HEAD_DIM - half // 2, half // 2),
            (cos, jnp.where(low, -sin, 0.0), jnp.where(low, 0.0, sin)))


def _prep_kernel(x_ref, gain_ref, *refs, shifts):
    *tab_refs, o_ref = refs
    tables = [t[...] for t in tab_refs]
    averager = jnp.full((HEAD_DIM, HEAD_DIM), 1.0 / HEAD_DIM, BF16)
    for h0 in range(0, x_ref.shape[1], HEAD_DIM):
        head = slice(h0, h0 + HEAD_DIM)
        x = x_ref[:, head].astype(F32)
        ms = jnp.dot((x * x).astype(BF16), averager, preferred_element_type=F32)
        y = x * lax.rsqrt(ms + EPS) * gain_ref[:, head]
        if shifts:
            r = y * tables[0]
            for sh, t in zip(shifts, tables[1:]):
                r = r + pltpu.roll(y, sh, 1) * t
            y = r
        o_ref[:, head] = y.astype(o_ref.dtype)


def in_projection(normed, w_in, layer, segments, rope, seq, tiles_per_group, v_tile, *, tn, name):
    gains = jnp.concatenate([jnp.tile(jnp.ones((HEAD_DIM,), F32) if g is None else g.astype(F32), n)
                             for g, n in segments])[None]
    shifts, tables = rope if rope is not None else ((), ())
    a, norm_in = normed
    proj = matmul(a, w_in, layer=layer, out_dtype=BF16, norm_in=norm_in, tk=a.shape[1], name=name)
    t, n = proj.shape
    ts = _tile(seq, 1024, 16)
    per_seq = seq // ts
    qk_tiles = n // tn // tiles_per_group * v_tile
    tile = lambda j: j // v_tile * tiles_per_group + j % v_tile
    return pl.pallas_call(
        functools.partial(_prep_kernel, shifts=shifts),
        out_shape=jax.ShapeDtypeStruct((t, n), BF16),
        grid=(t // ts, qk_tiles),
        in_specs=[pl.BlockSpec((ts, tn), lambda i, j: (i, tile(j))), pl.BlockSpec((1, tn), lambda i, j: (0, tile(j)))]
        + [pl.BlockSpec((ts, HEAD_DIM), lambda i, j: (i % per_seq, 0))] * len(tables),
        out_specs=pl.BlockSpec((ts, tn), lambda i, j: (i, tile(j))),
        input_output_aliases={0: 0},
        compiler_params=_params("parallel", "parallel"),
        name=name + "_prep",
    )(proj, gains, *[tab.astype(F32) for tab in tables])


def _stack_heads(x, n):
    return jnp.concatenate([x[:, h * HEAD_DIM:(h + 1) * HEAD_DIM] for h in range(n)], axis=0)


def _unstack_heads(x, n):
    rows = x.shape[0] // n
    return jnp.concatenate([x[h * rows:(h + 1) * rows] for h in range(n)], axis=1)


def _qk(q, k):
    return lax.dot_general(q, k, (((1,), (1,)), ((), ())), preferred_element_type=F32)


def _dense_gqa_kernel(q_ref, k_ref, v_ref, o_ref, *, group, chains):
    k, v = k_ref[0], v_ref[0]
    per = group // chains

    def scores(c):
        return _qk(_stack_heads(q_ref[0, :, c * per * HEAD_DIM:(c + 1) * per * HEAD_DIM], per), k)

    ahead = [scores(c) for c in range(min(2, chains))]
    for c in range(chains):
        cols = slice(c * per * HEAD_DIM, (c + 1) * per * HEAD_DIM)
        s = ahead.pop(0)
        if c + 2 < chains:
            ahead.append(scores(c + 2))
        m = jnp.max(s, axis=-1, keepdims=True)
        p = jnp.exp2(s - m)
        l = jnp.sum(p, axis=-1, keepdims=True)
        o = jnp.dot(p.astype(BF16), v, preferred_element_type=F32) / l
        o_ref[0, :, cols] = _unstack_heads(o, per).astype(o_ref.dtype)


def dense_gqa(proj, *, kv_heads, group, tq=256, chains=4):
    b, s, _ = proj.shape
    tq = _tile(s, tq, 16)
    w = group * HEAD_DIM
    k_block0 = kv_heads * group
    v_block0 = k_block0 + kv_heads
    return pl.pallas_call(
        functools.partial(_dense_gqa_kernel, group=group, chains=chains),
        out_shape=jax.ShapeDtypeStruct((b, s, kv_heads * w), BF16),
        grid=(b, kv_heads, s // tq),
        in_specs=[
            pl.BlockSpec((1, tq, w), lambda bi, g, i: (bi, i, g)),
            pl.BlockSpec((1, s, HEAD_DIM), lambda bi, g, i: (bi, 0, k_block0 + g)),
            pl.BlockSpec((1, s, HEAD_DIM), lambda bi, g, i: (bi, 0, v_block0 + g)),
        ],
        out_specs=pl.BlockSpec((1, tq, w), lambda bi, g, i: (bi, i, g)),
        compiler_params=_params("parallel", "parallel", "arbitrary"),
        name="dense_gqa",
    )(proj, proj, proj)


def _dilated_kernel(*refs, configs, seq, group):
    o_ref = refs[-1]
    i = pl.program_id(2)
    tq = o_ref.shape[1]

    chains = []
    for gi, (window, dil) in enumerate(configs):
        reach = (window // (2 * dil)) * dil
        pad = -(-reach // tq) * tq
        wlen = min(seq, tq + 2 * pad)
        start = pl.multiple_of(jnp.clip(i * tq - pad, 0, seq - wlen), tq)
        u = (start + lax.broadcasted_iota(jnp.int32, (tq, wlen), 1)
             - i * tq - lax.broadcasted_iota(jnp.int32, (tq, wlen), 0) + reach)
        if dil > 1:
            u = u + ((u & (dil - 1)) << RESIDUE_LIFT_BITS)
        ok = lax.bitcast_convert_type(u, jnp.uint32) <= jnp.uint32(2 * reach)
        bias = jnp.where(ok, 0.0, MASKED)
        per = group if wlen <= 4 * tq else max(1, group // 2)
        for h0 in range(0, group, per):
            chains.append((gi, h0, per, start, wlen, bias))

    def scores(c):
        gi, h0, per, start, wlen, _ = chains[c]
        q_ref, k_ref = refs[3 * gi], refs[3 * gi + 1]
        q = _stack_heads(q_ref[0, :, h0 * HEAD_DIM:(h0 + per) * HEAD_DIM], per)
        return _qk(q, k_ref[0, pl.ds(start, wlen), :])

    outs = [[None] * group for _ in configs]
    lses = [[None] * group for _ in configs]
    ahead = [scores(c) for c in range(min(2, len(chains)))]
    for c, (gi, h0, per, start, wlen, bias) in enumerate(chains):
        s = ahead.pop(0)
        if c + 2 < len(chains):
            ahead.append(scores(c + 2))
        s = (s.reshape(per, tq, wlen) + bias[None]).reshape(per * tq, wlen)
        m = jnp.max(s, axis=-1, keepdims=True)
        p = jnp.exp2(s - m)
        l = jnp.sum(p, axis=-1, keepdims=True)
        v_ref = refs[3 * gi + 2]
        o = jnp.dot(p.astype(BF16), v_ref[0, pl.ds(start, wlen), :], preferred_element_type=F32) / l
        lse = m + jnp.log2(l)
        for h in range(per):
            outs[gi][h0 + h] = o[h * tq:(h + 1) * tq]
            lses[gi][h0 + h] = lse[h * tq:(h + 1) * tq]

    for h in range(group):
        lse_h = [lses[gi][h] for gi in range(len(configs))]
        top = functools.reduce(jnp.maximum, lse_h)
        ws = [jnp.exp2(x - top) for x in lse_h]
        o = sum(w * outs[gi][h] for gi, w in enumerate(ws)) / sum(ws)
        o_ref[0, :, h * HEAD_DIM:(h + 1) * HEAD_DIM] = o.astype(o_ref.dtype)


def dilated_attention(proj, *, configs=B_CONFIGS, tq=256):
    b, s, _ = proj.shape
    tq = _tile(s, tq, LANES)
    for window, dil in configs:
        assert dil & (dil - 1) == 0, "the residue test uses a power-of-two dilation"
        assert 2 * window + s < 1 << RESIDUE_LIFT_BITS, "a lifted residue must clear every in-range offset"
    group = B_HEADS // B_KV_HEADS
    w = group * HEAD_DIM
    per_group_blocks = B_HEADS + 2 * B_KV_HEADS
    in_specs, args = [], []
    for gi in range(len(configs)):
        q0 = gi * per_group_blocks // group
        k0 = gi * per_group_blocks + B_HEADS
        in_specs += [
            pl.BlockSpec((1, tq, w), lambda bi, h, i, q0=q0: (bi, i, q0 + h)),
            pl.BlockSpec((1, s, HEAD_DIM), lambda bi, h, i, k0=k0: (bi, 0, k0 + h)),
            pl.BlockSpec((1, s, HEAD_DIM), lambda bi, h, i, k0=k0: (bi, 0, k0 + B_KV_HEADS + h)),
        ]
        args += [proj, proj, proj]
    return pl.pallas_call(
        functools.partial(_dilated_kernel, configs=configs, seq=s, group=group),
        out_shape=jax.ShapeDtypeStruct((b, s, B_HEADS * HEAD_DIM), BF16),
        grid=(b, B_KV_HEADS, s // tq),
        in_specs=in_specs,
        out_specs=pl.BlockSpec((1, tq, w), lambda bi, h, i: (bi, i, h)),
        compiler_params=_params("parallel", "parallel", "arbitrary"),
        name="dilated_attention",
    )(*args)


def _differential_kernel(q_ref, k_ref, v_ref, lam_ref, sub_ref, o_ref, *, lambda_init, chains):
    lp = lam_ref[...]
    lam = (jnp.exp(jnp.sum(lp[0:1] * lp[1:2], axis=-1, keepdims=True))
           - jnp.exp(jnp.sum(lp[2:3] * lp[3:4], axis=-1, keepdims=True)) + lambda_init)
    k, v = k_ref[0], v_ref[0]
    rows = q_ref.shape[1] // chains
    def scores(c):
        q = q_ref[0, c * rows:(c + 1) * rows, :]
        return [_qk(q[:, mi * HEAD_DIM:(mi + 1) * HEAD_DIM], k[:, mi * HEAD_DIM:(mi + 1) * HEAD_DIM])
                for mi in range(2)]

    nxt = scores(0)
    for c in range(chains):
        cur = nxt
        if c + 1 < chains:
            nxt = scores(c + 1)
        probs = []
        for s in cur:
            m = jnp.max(s, axis=-1, keepdims=True)
            p = jnp.exp2(s - m)
            probs.append((p, jnp.sum(p, axis=-1, keepdims=True)))
        (p0, l0), (p1, l1) = probs
        a = p0 - p1 * (lam * l0 / l1)
        o = jnp.dot(a.astype(BF16), v, preferred_element_type=F32) / l0
        ms = jnp.mean(o * o, axis=-1, keepdims=True)
        o = o * lax.rsqrt(ms + EPS) * sub_ref[...] * (1.0 - lambda_init)
        o_ref[0, c * rows:(c + 1) * rows, :] = o.astype(o_ref.dtype)


def differential_attention(proj, lam_params, subln, lambda_init, *, tq=1024, chains=4):
    b, s, _ = proj.shape
    tq = _tile(s, tq, 16)
    w = 2 * HEAD_DIM
    return pl.pallas_call(
        functools.partial(_differential_kernel, lambda_init=lambda_init, chains=chains),
        out_shape=jax.ShapeDtypeStruct((b, s, C_HEADS * w), BF16),
        grid=(b, C_HEADS, s // tq),
        in_specs=[
            pl.BlockSpec((1, tq, w), lambda bi, h, i: (bi, i, h)),
            pl.BlockSpec((1, s, w), lambda bi, h, i: (bi, 0, C_HEADS + h)),
            pl.BlockSpec((1, s, w), lambda bi, h, i: (bi, 0, 2 * C_HEADS + h)),
            pl.BlockSpec((4, HEAD_DIM), lambda bi, h, i: (0, 0)),
            pl.BlockSpec((1, w), lambda bi, h, i: (0, 0)),
        ],
        out_specs=pl.BlockSpec((1, tq, w), lambda bi, h, i: (bi, i, h)),
        compiler_params=_params("parallel", "parallel", "arbitrary"),
        name="differential_attention",
    )(proj, proj, proj, lam_params.astype(F32), subln.reshape(1, w).astype(F32))


def _na_bias_kernel(rpb_ref, o_ref):
    rows = lax.broadcasted_iota(jnp.int32, (GRID_W, LANES), 0)
    lane = lax.broadcasted_iota(jnp.int32, (GRID_W, LANES), 1)
    n_rel = 2 * NA_ROWS - 1
    toeplitz = []
    for rho in range(n_rel):
        z = jnp.broadcast_to(rpb_ref[0, rho:rho + 1, :], (GRID_W, LANES))
        z = pltpu.roll(z, LANES - (NA_COLS - 1), 1)
        bit = 1
        while bit < GRID_W:
            z = jnp.where((rows & bit) != 0, pltpu.roll(z, bit, 1), z)
            bit *= 2
        toeplitz.append(z)
    kc = lane & (GRID_W - 1)
    first = jnp.clip(rows - NA_COLS // 2, 0, GRID_W - NA_COLS)
    inside = (kc >= first) & (kc < first + NA_COLS)
    for d in range(NA_ROWS):
        for pair in range(NA_ROWS // 2):
            even = toeplitz[d + 2 * pair]
            odd = pltpu.roll(toeplitz[d + 2 * pair + 1], GRID_W, 1)
            tile = jnp.where(lane < GRID_W, even, odd)
            o_ref[d, 0, :, pair * LANES:(pair + 1) * LANES] = jnp.where(inside, tile * LOG2E, MASKED)


def na_bias_table(rpb):
    h = rpb.shape[0]
    padded = jnp.zeros((h, 2 * NA_ROWS, LANES), F32).at[:, :2 * NA_ROWS - 1, :2 * NA_COLS - 1].set(rpb.astype(F32))
    return pl.pallas_call(
        _na_bias_kernel,
        out_shape=jax.ShapeDtypeStruct((NA_ROWS, h, GRID_W, NA_ROWS * GRID_W), F32),
        grid=(h,),
        in_specs=[pl.BlockSpec((1, 2 * NA_ROWS, LANES), lambda i: (i, 0, 0))],
        out_specs=pl.BlockSpec((NA_ROWS, 1, GRID_W, NA_ROWS * GRID_W), lambda i: (0, i, 0, 0)),
        compiler_params=_params("parallel"),
        name="na_bias_table",
    )(padded)


def _neighbourhood_kernel(q_ref, k_ref, v_ref, b_ref, o_ref, *, n_rows, group):
    win = NA_ROWS * GRID_W

    def row(r, carry):
        first = jnp.clip(r - NA_ROWS // 2, 0, n_rows - NA_ROWS)
        q0 = pl.multiple_of(r * GRID_W, GRID_W)
        k0 = pl.multiple_of(first * GRID_W, GRID_W)
        q = _stack_heads(q_ref[0, pl.ds(q0, GRID_W), :], group)
        s = _qk(q, k_ref[0, pl.ds(k0, win), :])
        s = (s.reshape(group, GRID_W, win) + b_ref[first - r + NA_ROWS - 1]).reshape(group * GRID_W, win)
        m = jnp.max(s, axis=-1, keepdims=True)
        p = jnp.exp2(s - m)
        l = jnp.sum(p, axis=-1, keepdims=True)
        o = jnp.dot(p.astype(BF16), v_ref[0, pl.ds(k0, win), :], preferred_element_type=F32) / l
        o_ref[0, pl.ds(q0, GRID_W), :] = _unstack_heads(o, group).astype(o_ref.dtype)
        return carry

    lax.fori_loop(0, n_rows, row, 0, unroll=16)


def neighbourhood_attention(proj, bias, *, kv_heads, group):
    b, s, _ = proj.shape
    n_rows = s // GRID_W
    assert n_rows >= NA_ROWS, "the key window is NA_ROWS grid rows tall"
    w = group * HEAD_DIM
    k_block0 = kv_heads * group
    v_block0 = k_block0 + kv_heads
    return pl.pallas_call(
        functools.partial(_neighbourhood_kernel, n_rows=n_rows, group=group),
        out_shape=jax.ShapeDtypeStruct((b, s, kv_heads * w), BF16),
        grid=(b, kv_heads),
        in_specs=[
            pl.BlockSpec((1, s, w), lambda bi, g: (bi, 0, g)),
            pl.BlockSpec((1, s, HEAD_DIM), lambda bi, g: (bi, 0, k_block0 + g)),
            pl.BlockSpec((1, s, HEAD_DIM), lambda bi, g: (bi, 0, v_block0 + g)),
            pl.BlockSpec((NA_ROWS, group, GRID_W, NA_ROWS * GRID_W), lambda bi, g: (0, g, 0, 0)),
        ],
        out_specs=pl.BlockSpec((1, s, w), lambda bi, g: (bi, 0, g)),
        compiler_params=_params("parallel", "arbitrary"),
        name="neighbourhood_attention",
    )(proj, proj, proj, bias)


IN_TILE = 1024


def _with_next_norm(outputs):
    x, xg, sq = outputs
    return x, (xg, (sq, x.shape[1]))


def _out_projection(o, x, w_out, layer, next_gain, name):
    tn = 1024 // len(_row_parts(x))
    return _with_next_norm(matmul(o.reshape(-1, o.shape[-1]), w_out, layer=layer, out_dtype=F32, res=x,
                                  norm_out=next_gain, tk=o.shape[-1], tn=tn, name=name))


def mixer_axial_gqa(normed, x, w_in, w_out, layer, next_gain, q_gain, k_gain, b, s):
    segments = [(q_gain * Q_SCALE, A_HEADS), (k_gain, A_KV_HEADS), (None, A_KV_HEADS)]
    tiles = (A_HEADS + 2 * A_KV_HEADS) * HEAD_DIM // IN_TILE
    proj = in_projection(normed, w_in, layer, segments, axial_rope_tables(s), s, tiles,
                         (A_HEADS + A_KV_HEADS) * HEAD_DIM // IN_TILE, tn=IN_TILE, name="a_in")
    o = dense_gqa(proj.reshape(b, s, -1), kv_heads=A_KV_HEADS, group=A_HEADS // A_KV_HEADS)
    return _out_projection(o, x, w_out, layer, next_gain, "a_out")


def mixer_dilated(normed, x, w_in, w_out, layer, next_gain, q_gain, k_gain, b, s):
    segments = []
    for gi in range(len(B_CONFIGS)):
        segments += [(q_gain[gi] * Q_SCALE, B_HEADS), (k_gain[gi], B_KV_HEADS), (None, B_KV_HEADS)]
    tn = B_KV_HEADS * HEAD_DIM
    tiles = (B_HEADS + 2 * B_KV_HEADS) * HEAD_DIM // tn
    proj = in_projection(normed, w_in, layer, segments, full_rope_tables(s), s, tiles, tiles - 1, tn=tn, name="b_in")
    o = dilated_attention(proj.reshape(b, s, -1))
    return _out_projection(o, x, w_out, layer, next_gain, "b_out")


def mixer_differential(normed, x, w_in, w_out, layer, next_gain, q_gain, k_gain, lam_params, subln, lambda_init,
                       b, s):
    segments = [(q_gain * Q_SCALE, 2 * C_HEADS), (k_gain, 2 * C_HEADS), (None, 2 * C_HEADS)]
    tiles = 3 * 2 * C_HEADS * HEAD_DIM // IN_TILE
    proj = in_projection(normed, w_in, layer, segments, full_rope_tables(s), s, tiles, 2 * tiles // 3,
                         tn=IN_TILE, name="c_in")
    o = differential_attention(proj.reshape(b, s, -1), lam_params, subln, lambda_init)
    return _out_projection(o, x, w_out, layer, next_gain, "c_out")


def mixer_neighbourhood(normed, x, w_in, w_out, layer, next_gain, q_gain, k_gain, rpb, b, s):
    segments = [(q_gain * Q_SCALE, D_HEADS), (k_gain, D_KV_HEADS), (None, D_KV_HEADS)]
    tiles = (D_HEADS + 2 * D_KV_HEADS) * HEAD_DIM // IN_TILE
    proj = in_projection(normed, w_in, layer, segments, None, s, tiles,
                         (D_HEADS + D_KV_HEADS) * HEAD_DIM // IN_TILE, tn=IN_TILE, name="d_in")
    o = neighbourhood_attention(proj.reshape(b, s, -1), na_bias_table(rpb),
                                kv_heads=D_KV_HEADS, group=D_HEADS // D_KV_HEADS)
    return _out_projection(o, x, w_out, layer, next_gain, "d_out")


def squared_relu_mlp(normed, x, w_up, w_down, next_gain=None, next_weights=(), split=None):
    a, norm_in = normed
    u, *cast_weights = _row_parts(matmul(a, w_up, out_dtype=BF16, act="relu2", norm_in=norm_in, casts=next_weights,
                                         tk=a.shape[1], name="mlp_up"))
    down = functools.partial(matmul, u, w_down, out_dtype=F32, res=x, tk=4096, name="mlp_down")
    if split is None:
        return (*_with_next_norm(down(norm_out=next_gain)), cast_weights)
    return down(rows=(0, split)), down(rows=(split, u.shape[0] - split))


def trunk(x, b, s, ln_mix, ln_ffn, w_up, w_down, mixers, split):
    depth = ln_mix.shape[0]
    normed = (rmsnorm(x, ln_mix[0]), None)
    mlp_weights = [w_up[0].astype(BF16), w_down[0].astype(BF16)]
    for i in range(depth):
        m, j = i % N_MIXERS, i // N_MIXERS
        p = mixers[m]
        common = (normed, x, p["w_in"], p["w_out"], j, ln_ffn[i],
                  p["q_gain"][j].astype(F32), p["k_gain"][j].astype(F32))
        if m == 0:
            x, normed = mixer_axial_gqa(*common, b, s)
        elif m == 1:
            x, normed = mixer_dilated(*common, b, s)
        elif m == 2:
            lambda_init = 0.8 - 0.6 * math.exp(-0.3 * i)
            x, normed = mixer_differential(*common, p["lam"][j], p["subln"][j], lambda_init, b, s)
        else:
            x, normed = mixer_neighbourhood(*common, p["rpb"][j], b, s)
        if i == depth - 1:
            return squared_relu_mlp(normed, x, *mlp_weights, split=split)
        x, normed, mlp_weights = squared_relu_mlp(normed, x, *mlp_weights, next_gain=ln_mix[i + 1],
                                                  next_weights=[(w_up, i + 1), (w_down, i + 1)])


def kernel(x_prompt, x_sample, ln_mix, ln_ffn, w_up, w_down, a_w_in, a_w_out, a_q_gain, a_k_gain, b_w_in, b_w_out, b_q_gain, b_k_gain, c_w_in, c_w_out, c_q_gain, c_k_gain, c_lambda, c_subln, d_w_in, d_w_out, d_q_gain, d_k_gain, d_rpb):
    bp, s, d = x_prompt.shape
    bs = x_sample.shape[0]
    assert x_sample.shape[1:] == (s, d)
    b = bp + bs
    x = (x_prompt.reshape(bp * s, d), x_sample.reshape(bs * s, d))
    cast = lambda w: w.astype(BF16)
    mixers = [
        dict(w_in=cast(a_w_in), w_out=cast(a_w_out), q_gain=a_q_gain, k_gain=a_k_gain),
        dict(w_in=cast(b_w_in), w_out=cast(b_w_out), q_gain=b_q_gain, k_gain=b_k_gain),
        dict(w_in=cast(c_w_in), w_out=cast(c_w_out), q_gain=c_q_gain, k_gain=c_k_gain,
             lam=c_lambda, subln=c_subln),
        dict(w_in=cast(d_w_in), w_out=cast(d_w_out), q_gain=d_q_gain, k_gain=d_k_gain, rpb=d_rpb),
    ]
    y_prompt, y_sample = trunk(x, b, s, ln_mix, ln_ffn, w_up, w_down, mixers, split=bp * s)
    return (y_prompt.reshape(bp, s, d), y_sample.reshape(bs, s, d))
```

```python
import functools
import math

import jax
import jax.numpy as jnp
from jax import lax
from jax.experimental import pallas as pl
from jax.experimental.pallas import tpu as pltpu

F32 = jnp.float32
BF16 = jnp.bfloat16

HEAD_DIM = 128
GRID_W = 64
ROPE_THETA = 10000.0
EPS = 1e-6
LOG2E = math.log2(math.e)
Q_SCALE = HEAD_DIM ** -0.5 * LOG2E
N_MIXERS = 4
A_HEADS, A_KV_HEADS = 32, 8
B_CONFIGS = ((128, 1), (512, 4), (2048, 16))
B_HEADS, B_KV_HEADS = 16, 4
C_HEADS = 16
D_HEADS, D_KV_HEADS = 32, 8
NA_ROWS, NA_COLS = 8, 16
MASKED = -1e30
RESIDUE_LIFT_BITS = 16

LANES = 128
V7X_VMEM_BYTES = 64 * 1024 * 1024
VMEM_LIMIT_BYTES = V7X_VMEM_BYTES - 8 * 1024 * 1024
MATMUL_TEMP_BYTES = 3 * 1024 * 1024
CAST_ROWS, CAST_COLS = 256, 1024


def _tile(dim, pref, quantum):
    if dim <= pref:
        return dim
    t = (pref // quantum) * quantum
    while dim % t:
        t -= quantum
    return t


def _params(*semantics):
    return pltpu.CompilerParams(dimension_semantics=semantics, vmem_limit_bytes=VMEM_LIMIT_BYTES)


def _row_parts(x):
    return x if isinstance(x, tuple) else (x,)


def _row_part_specs(parts, tile_rows, width, row_tile, col_tile):
    specs, first = [], 0
    for p in parts:
        n = p.shape[0] // tile_rows
        specs.append(pl.BlockSpec(
            (tile_rows, width),
            lambda *g, first=first, n=n: (jnp.clip(row_tile(*g) - first, 0, n - 1), col_tile(*g))))
        first += n
    return specs


def _read_part(refs, part_tiles, row_tile, read):
    value, first = None, 0
    for ref, n in zip(refs, part_tiles):
        value = read(ref) if value is None else jnp.where(row_tile >= first, read(ref), value)
        first += n
    return value


def _rmsnorm_kernel(*refs, part_tiles):
    *x_refs, g_ref, o_ref = refs

    def emit(x_ref):
        x = x_ref[...]
        ms = jnp.mean(x * x, axis=-1, keepdims=True)
        o_ref[...] = (x * lax.rsqrt(ms + EPS) * g_ref[...]).astype(o_ref.dtype)

    first = 0
    for x_ref, n in zip(x_refs, part_tiles):
        i = pl.program_id(0)
        pl.when((i >= first) & (i < first + n))(functools.partial(emit, x_ref))
        first += n


def rmsnorm(x, gain):
    parts = _row_parts(x)
    d = parts[0].shape[1]
    t = sum(p.shape[0] for p in parts)
    tr = _tile(math.gcd(*[p.shape[0] for p in parts]), 512, 8)
    return pl.pallas_call(
        functools.partial(_rmsnorm_kernel, part_tiles=[p.shape[0] // tr for p in parts]),
        out_shape=jax.ShapeDtypeStruct((t, d), BF16),
        grid=(t // tr,),
        in_specs=_row_part_specs(parts, tr, d, lambda i: i, lambda i: 0) + [pl.BlockSpec((1, d), lambda i: (0, 0))],
        out_specs=pl.BlockSpec((tr, d), lambda i: (i, 0)),
        compiler_params=_params("parallel"),
        name="rmsnorm",
    )(*parts, gain.reshape(1, d).astype(F32))


def _lane_partial_sums(v):
    return functools.reduce(lambda a, b: a + b, [v[:, c:c + LANES] for c in range(0, v.shape[1], LANES)])


def _matmul_kernel(a_ref, w_ref, *refs, nk, act, res_tiles, first_tile, chunk, norm_in, norm_out, norm_dim,
                   n_casts):
    refs = list(refs)
    n_out = (3 if norm_out else 1) + n_casts
    outs, refs = refs[len(refs) - n_out:], refs[:len(refs) - n_out]
    o_ref, norm_out_refs, cast_out_refs = outs[0], outs[1:n_out - n_casts], outs[n_out - n_casts:]
    cast_in_refs = [refs.pop() for _ in range(n_casts)][::-1]
    gain_ref = refs.pop() if norm_out else None
    sq_ref = refs.pop() if norm_in else None
    res_refs = refs
    k = pl.program_id(2)

    for src_ref, dst_ref in zip(cast_in_refs, cast_out_refs):
        dst_ref[...] = src_ref[...].astype(dst_ref.dtype)

    def residual(cols):
        return _read_part(res_refs, res_tiles, first_tile + pl.program_id(0), lambda ref: ref[:, cols])

    def emit_norm(x, cols, first):
        xg_ref, sq_out_ref = norm_out_refs
        xg_ref[:, cols] = (x * gain_ref[:, cols]).astype(xg_ref.dtype)
        sq = _lane_partial_sums(x * x)
        sq_out_ref[...] = sq if first else sq_out_ref[...] + sq

    if norm_in:
        scale = lax.rsqrt(jnp.sum(sq_ref[...], axis=-1, keepdims=True) * (1.0 / norm_dim) + EPS)

    if nk > 1:
        @pl.when(k == 0)
        def _():
            o_ref[...] = residual(slice(None)) if res_refs else jnp.zeros_like(o_ref)

    for c in range(o_ref.shape[1] // chunk):
        cols = slice(c * chunk, (c + 1) * chunk)
        part = jnp.dot(a_ref[...], w_ref[:, cols], preferred_element_type=F32)
        if nk > 1:
            part = o_ref[:, cols] + part
            o_ref[:, cols] = part
            if norm_out:
                emit_norm(part, cols, c == 0)
            continue
        if norm_in:
            part = part * scale
        if res_refs:
            part = part + residual(cols)
        if act == "relu2":
            r = jnp.maximum(part, 0.0)
            part = r * r
        o_ref[:, cols] = part.astype(o_ref.dtype)
        if norm_out:
            emit_norm(part, cols, c == 0)


def matmul(a, w, *, out_dtype, layer=None, act=None, res=None, rows=None, norm_in=None, norm_out=None,
           casts=(), tm=1024, tn=1024, tk=2048, chunk=256, name="matmul"):
    row0, m = rows if rows is not None else (0, a.shape[0])
    kdim, n = w.shape[-2:]
    res_parts = _row_parts(res) if res is not None else ()
    tm = _tile(math.gcd(m, row0, *[p.shape[0] for p in res_parts[:-1]]), tm, 8)
    tn, tk = _tile(n, tn, LANES), _tile(kdim, tk, LANES)
    cast_rows_of = []
    for src, _ in casts:
        cast_rows = CAST_ROWS
        while (src.shape[-2] // cast_rows) * (src.shape[-1] // CAST_COLS) > (m // tm) * (n // tn):
            cast_rows *= 2
        cast_rows_of.append(cast_rows)

    def window_bytes(tk):
        per_step = tm * tk * 2 + tk * tn * 2 + tm * tn * (4 * len(res_parts) + jnp.dtype(out_dtype).itemsize)
        if norm_out is not None:
            per_step += tm * tn * 2 + tm * LANES * 4
        if norm_in is not None:
            per_step += tm * norm_in[0].shape[1] * 4
        per_step += sum(cast_rows_of) * CAST_COLS * (4 + 2)
        return 2 * per_step

    while window_bytes(tk) > VMEM_LIMIT_BYTES - MATMUL_TEMP_BYTES and tk % (2 * LANES) == 0:
        tk //= 2
    nk = kdim // tk
    t0 = row0 // tm
    chunk = min(chunk, tn)
    assert nk == 1 or (out_dtype == F32 and act is None), "a split K accumulates in the f32 output block"
    if w.ndim == 3:
        w_spec = pl.BlockSpec((None, tk, tn), lambda i, j, k: (layer, k, j))
    else:
        w_spec = pl.BlockSpec((tk, tn), lambda i, j, k: (k, j))
    in_specs = [pl.BlockSpec((tm, tk), lambda i, j, k: (t0 + i, k)), w_spec]
    args = [a, w]
    in_specs += _row_part_specs(res_parts, tm, tn, lambda i, j, k: t0 + i, lambda i, j, k: j)
    args += res_parts
    norm_dim = None
    if norm_in is not None:
        sq, norm_dim = norm_in
        assert nk == 1, "the row scale is applied where the whole product is formed"
        in_specs.append(pl.BlockSpec((tm, sq.shape[1]), lambda i, j, k: (t0 + i, 0)))
        args.append(sq)
    out_shape = jax.ShapeDtypeStruct((m, n), out_dtype)
    out_specs = pl.BlockSpec((tm, tn), lambda i, j, k: (i, j))
    if norm_out is not None:
        assert out_dtype == F32
        in_specs.append(pl.BlockSpec((1, tn), lambda i, j, k: (0, j)))
        args.append(norm_out.reshape(1, n).astype(F32))
        out_shape = (out_shape, jax.ShapeDtypeStruct((m, n), BF16),
                     jax.ShapeDtypeStruct((m, n // tn * LANES), F32))
        out_specs = (out_specs, pl.BlockSpec((tm, tn), lambda i, j, k: (i, j)),
                     pl.BlockSpec((tm, LANES), lambda i, j, k: (i, j)))
    grid = (m // tm, n // tn, nk)
    for (src, src_layer), cast_rows in zip(casts, cast_rows_of):
        rows_, cols_ = src.shape[-2] // cast_rows, src.shape[-1] // CAST_COLS

        def block(i, j, k, cols_=cols_, last=rows_ * cols_ - 1):
            step = jnp.minimum((i * grid[1] + j) * grid[2] + k, last)
            return step // cols_, step % cols_

        in_specs.append(pl.BlockSpec((None, cast_rows, CAST_COLS),
                                     lambda i, j, k, block=block, src_layer=src_layer: (src_layer, *block(i, j, k))))
        args.append(src)
        out_shape = (*_row_parts(out_shape), jax.ShapeDtypeStruct(src.shape[-2:], BF16))
        out_specs = (*_row_parts(out_specs), pl.BlockSpec((cast_rows, CAST_COLS), block))
    return pl.pallas_call(
        functools.partial(_matmul_kernel, nk=nk, act=act, chunk=chunk, first_tile=t0,
                          res_tiles=[p.shape[0] // tm for p in res_parts], n_casts=len(casts),
                          norm_in=norm_in is not None, norm_out=norm_out is not None, norm_dim=norm_dim),
        out_shape=out_shape,
        grid=grid,
        in_specs=in_specs,
        out_specs=out_specs,
        compiler_params=_params("parallel", "parallel", "arbitrary"),
        name=name,
    )(*args)


def rope_tables(pos, dim):
    inv = ROPE_THETA ** (-jnp.arange(0, dim, 2, dtype=F32) / dim)
    ang = pos.astype(F32)[:, None] * inv[None, :]
    ang = jnp.concatenate([ang, ang], axis=-1)
    return jnp.cos(ang), jnp.sin(ang)


def full_rope_tables(s):
    cos, sin = rope_tables(jnp.arange(s), HEAD_DIM)
    lane = jnp.arange(HEAD_DIM)
    sin_signed = jnp.where(lane < HEAD_DIM // 2, -sin, sin)
    return ((HEAD_DIM // 2,), (cos, sin_signed))


def axial_rope_tables(s):
    half = HEAD_DIM // 2
    t = jnp.arange(s)
    cr, sr = rope_tables(t // GRID_W, half)
    cc, sc = rope_tables(t % GRID_W, half)
    cos = jnp.concatenate([cr, cc], axis=-1)
    sin = jnp.concatenate([sr, sc], axis=-1)
    low = (jnp.arange(HEAD_DIM) % half) < half // 2
    return ((HEAD_DIM - half // 2, half // 2),
            (cos, jnp.where(low, -sin, 0.0), jnp.where(low, 0.0, sin)))


def _prep_kernel(x_ref, gain_ref, *refs, shifts):
    *tab_refs, o_ref = refs
    tables = [t[...] for t in tab_refs]
    averager = jnp.full((HEAD_DIM, HEAD_DIM), 1.0 / HEAD_DIM, BF16)
    for h0 in range(0, x_ref.shape[1], HEAD_DIM):
        head = slice(h0, h0 + HEAD_DIM)
        x = x_ref[:, head].astype(F32)
        ms = jnp.dot((x * x).astype(BF16), averager, preferred_element_type=F32)
        y = x * lax.rsqrt(ms + EPS) * gain_ref[:, head]
        if shifts:
            r = y * tables[0]
            for sh, t in zip(shifts, tables[1:]):
                r = r + pltpu.roll(y, sh, 1) * t
            y = r
        o_ref[:, head] = y.astype(o_ref.dtype)


def in_projection(normed, w_in, layer, segments, rope, seq, tiles_per_group, v_tile, *, tn, name, casts=()):
    gains = jnp.concatenate([jnp.tile(jnp.ones((HEAD_DIM,), F32) if g is None else g.astype(F32), n)
                             for g, n in segments])[None]
    shifts, tables = rope if rope is not None else ((), ())
    a, norm_in = normed
    proj, *cast_weights = _row_parts(matmul(a, w_in, layer=layer, out_dtype=BF16, norm_in=norm_in, casts=casts,
                                            tk=a.shape[1], name=name))
    t, n = proj.shape
    ts = _tile(seq, 1024, 16)
    per_seq = seq // ts
    qk_tiles = n // tn // tiles_per_group * v_tile
    tile = lambda j: j // v_tile * tiles_per_group + j % v_tile
    prepped = pl.pallas_call(
        functools.partial(_prep_kernel, shifts=shifts),
        out_shape=jax.ShapeDtypeStruct((t, n), BF16),
        grid=(t // ts, qk_tiles),
        in_specs=[pl.BlockSpec((ts, tn), lambda i, j: (i, tile(j))), pl.BlockSpec((1, tn), lambda i, j: (0, tile(j)))]
        + [pl.BlockSpec((ts, HEAD_DIM), lambda i, j: (i % per_seq, 0))] * len(tables),
        out_specs=pl.BlockSpec((ts, tn), lambda i, j: (i, tile(j))),
        input_output_aliases={0: 0},
        compiler_params=_params("parallel", "parallel"),
        name=name + "_prep",
    )(proj, gains, *[tab.astype(F32) for tab in tables])
    return (prepped, cast_weights) if casts else prepped


def _stack_heads(x, n):
    return jnp.concatenate([x[:, h * HEAD_DIM:(h + 1) * HEAD_DIM] for h in range(n)], axis=0)


def _unstack_heads(x, n):
    rows = x.shape[0] // n
    return jnp.concatenate([x[h * rows:(h + 1) * rows] for h in range(n)], axis=1)


def _qk(q, k):
    return lax.dot_general(q, k, (((1,), (1,)), ((), ())), preferred_element_type=F32)


def _with_ones(v):
    return jnp.concatenate([v, jnp.ones_like(v)], axis=1)


def _normalised_pv(p, v_ones):
    r = jnp.dot(p, v_ones, preferred_element_type=F32)
    l = r[:, HEAD_DIM:]
    return r[:, :HEAD_DIM] / l, l


def _dense_gqa_kernel(q_ref, k_ref, v_ref, o_ref, *, group, chains):
    k, v = k_ref[0], _with_ones(v_ref[0])
    per = group // chains

    def scores(c):
        return _qk(_stack_heads(q_ref[0, :, c * per * HEAD_DIM:(c + 1) * per * HEAD_DIM], per), k)

    ahead = [scores(c) for c in range(min(2, chains))]
    for c in range(chains):
        cols = slice(c * per * HEAD_DIM, (c + 1) * per * HEAD_DIM)
        s = ahead.pop(0)
        if c + 2 < chains:
            ahead.append(scores(c + 2))
        m = jnp.max(s, axis=-1, keepdims=True)
        o, _ = _normalised_pv(jnp.exp2(s - m).astype(BF16), v)
        o_ref[0, :, cols] = _unstack_heads(o, per).astype(o_ref.dtype)


def dense_gqa(proj, *, kv_heads, group, tq=256, chains=4):
    b, s, _ = proj.shape
    tq = _tile(s, tq, 16)
    w = group * HEAD_DIM
    k_block0 = kv_heads * group
    v_block0 = k_block0 + kv_heads
    return pl.pallas_call(
        functools.partial(_dense_gqa_kernel, group=group, chains=chains),
        out_shape=jax.ShapeDtypeStruct((b, s, kv_heads * w), BF16),
        grid=(b, kv_heads, s // tq),
        in_specs=[
            pl.BlockSpec((1, tq, w), lambda bi, g, i: (bi, i, g)),
            pl.BlockSpec((1, s, HEAD_DIM), lambda bi, g, i: (bi, 0, k_block0 + g)),
            pl.BlockSpec((1, s, HEAD_DIM), lambda bi, g, i: (bi, 0, v_block0 + g)),
        ],
        out_specs=pl.BlockSpec((1, tq, w), lambda bi, g, i: (bi, i, g)),
        compiler_params=_params("parallel", "parallel", "arbitrary"),
        name="dense_gqa",
    )(proj, proj, proj)


def _dilated_kernel(*refs, configs, seq, group):
    o_ref = refs[-1]
    i = pl.program_id(2)
    tq = o_ref.shape[1]

    chains = []
    for gi, (window, dil) in enumerate(configs):
        reach = (window // (2 * dil)) * dil
        pad = -(-reach // tq) * tq
        wlen = min(seq, tq + 2 * pad)
        start = pl.multiple_of(jnp.clip(i * tq - pad, 0, seq - wlen), tq)
        u = (start + lax.broadcasted_iota(jnp.int32, (tq, wlen), 1)
             - i * tq - lax.broadcasted_iota(jnp.int32, (tq, wlen), 0) + reach)
        if dil > 1:
            u = u + ((u & (dil - 1)) << RESIDUE_LIFT_BITS)
        ok = lax.bitcast_convert_type(u, jnp.uint32) <= jnp.uint32(2 * reach)
        bias = jnp.where(ok, 0.0, MASKED)
        per = group if wlen <= 4 * tq else max(1, group // 2)
        for h0 in range(0, group, per):
            chains.append((gi, h0, per, start, wlen, bias))

    def scores(c):
        gi, h0, per, start, wlen, _ = chains[c]
        q_ref, k_ref = refs[3 * gi], refs[3 * gi + 1]
        q = _stack_heads(q_ref[0, :, h0 * HEAD_DIM:(h0 + per) * HEAD_DIM], per)
        return _qk(q, k_ref[0, pl.ds(start, wlen), :])

    outs = [[None] * group for _ in configs]
    lses = [[None] * group for _ in configs]
    ahead = [scores(c) for c in range(min(2, len(chains)))]
    for c, (gi, h0, per, start, wlen, bias) in enumerate(chains):
        s = ahead.pop(0)
        if c + 2 < len(chains):
            ahead.append(scores(c + 2))
        s = (s.reshape(per, tq, wlen) + bias[None]).reshape(per * tq, wlen)
        m = jnp.max(s, axis=-1, keepdims=True)
        v_ref = refs[3 * gi + 2]
        o, l = _normalised_pv(jnp.exp2(s - m).astype(BF16), _with_ones(v_ref[0, pl.ds(start, wlen), :]))
        lse = m + jnp.log2(l[:, :1])
        for h in range(per):
            outs[gi][h0 + h] = o[h * tq:(h + 1) * tq]
            lses[gi][h0 + h] = lse[h * tq:(h + 1) * tq]

    for h in range(group):
        lse_h = [lses[gi][h] for gi in range(len(configs))]
        top = functools.reduce(jnp.maximum, lse_h)
        ws = [jnp.exp2(x - top) for x in lse_h]
        o = sum(w * outs[gi][h] for gi, w in enumerate(ws)) / sum(ws)
        o_ref[0, :, h * HEAD_DIM:(h + 1) * HEAD_DIM] = o.astype(o_ref.dtype)


def dilated_attention(proj, *, configs=B_CONFIGS, tq=256):
    b, s, _ = proj.shape
    tq = _tile(s, tq, LANES)
    for window, dil in configs:
        assert dil & (dil - 1) == 0, "the residue test uses a power-of-two dilation"
        assert 2 * window + s < 1 << RESIDUE_LIFT_BITS, "a lifted residue must clear every in-range offset"
    group = B_HEADS // B_KV_HEADS
    w = group * HEAD_DIM
    per_group_blocks = B_HEADS + 2 * B_KV_HEADS
    in_specs, args = [], []
    for gi in range(len(configs)):
        q0 = gi * per_group_blocks // group
        k0 = gi * per_group_blocks + B_HEADS
        in_specs += [
            pl.BlockSpec((1, tq, w), lambda bi, h, i, q0=q0: (bi, i, q0 + h)),
            pl.BlockSpec((1, s, HEAD_DIM), lambda bi, h, i, k0=k0: (bi, 0, k0 + h)),
            pl.BlockSpec((1, s, HEAD_DIM), lambda bi, h, i, k0=k0: (bi, 0, k0 + B_KV_HEADS + h)),
        ]
        args += [proj, proj, proj]
    return pl.pallas_call(
        functools.partial(_dilated_kernel, configs=configs, seq=s, group=group),
        out_shape=jax.ShapeDtypeStruct((b, s, B_HEADS * HEAD_DIM), BF16),
        grid=(b, B_KV_HEADS, s // tq),
        in_specs=in_specs,
        out_specs=pl.BlockSpec((1, tq, w), lambda bi, h, i: (bi, i, h)),
        compiler_params=_params("parallel", "parallel", "arbitrary"),
        name="dilated_attention",
    )(*args)


def _differential_kernel(q_ref, k_ref, v_ref, lam_ref, sub_ref, o_ref, *, lambda_init, chains):
    lp = lam_ref[...]
    lam = (jnp.exp(jnp.sum(lp[0:1] * lp[1:2], axis=-1, keepdims=True))
           - jnp.exp(jnp.sum(lp[2:3] * lp[3:4], axis=-1, keepdims=True)) + lambda_init)
    k, v = k_ref[0], v_ref[0]
    rows = q_ref.shape[1] // chains
    def scores(c):
        q = q_ref[0, c * rows:(c + 1) * rows, :]
        return [_qk(q[:, mi * HEAD_DIM:(mi + 1) * HEAD_DIM], k[:, mi * HEAD_DIM:(mi + 1) * HEAD_DIM])
                for mi in range(2)]

    nxt = scores(0)
    for c in range(chains):
        cur = nxt
        if c + 1 < chains:
            nxt = scores(c + 1)
        probs = []
        for s in cur:
            m = jnp.max(s, axis=-1, keepdims=True)
            p = jnp.exp2(s - m)
            probs.append((p, jnp.sum(p, axis=-1, keepdims=True)))
        (p0, l0), (p1, l1) = probs
        a = p0 - p1 * (lam * l0 / l1)
        o = jnp.dot(a.astype(BF16), v, preferred_element_type=F32) / l0
        ms = jnp.mean(o * o, axis=-1, keepdims=True)
        o = o * lax.rsqrt(ms + EPS) * sub_ref[...] * (1.0 - lambda_init)
        o_ref[0, c * rows:(c + 1) * rows, :] = o.astype(o_ref.dtype)


def differential_attention(proj, lam_params, subln, lambda_init, *, tq=1024, chains=4):
    b, s, _ = proj.shape
    tq = _tile(s, tq, 16)
    w = 2 * HEAD_DIM
    return pl.pallas_call(
        functools.partial(_differential_kernel, lambda_init=lambda_init, chains=chains),
        out_shape=jax.ShapeDtypeStruct((b, s, C_HEADS * w), BF16),
        grid=(b, C_HEADS, s // tq),
        in_specs=[
            pl.BlockSpec((1, tq, w), lambda bi, h, i: (bi, i, h)),
            pl.BlockSpec((1, s, w), lambda bi, h, i: (bi, 0, C_HEADS + h)),
            pl.BlockSpec((1, s, w), lambda bi, h, i: (bi, 0, 2 * C_HEADS + h)),
            pl.BlockSpec((4, HEAD_DIM), lambda bi, h, i: (0, 0)),
            pl.BlockSpec((1, w), lambda bi, h, i: (0, 0)),
        ],
        out_specs=pl.BlockSpec((1, tq, w), lambda bi, h, i: (bi, i, h)),
        compiler_params=_params("parallel", "parallel", "arbitrary"),
        name="differential_attention",
    )(proj, proj, proj, lam_params.astype(F32), subln.reshape(1, w).astype(F32))


def _na_bias_kernel(rpb_ref, o_ref):
    rows = lax.broadcasted_iota(jnp.int32, (GRID_W, LANES), 0)
    lane = lax.broadcasted_iota(jnp.int32, (GRID_W, LANES), 1)
    n_rel = 2 * NA_ROWS - 1
    toeplitz = []
    for rho in range(n_rel):
        z = jnp.broadcast_to(rpb_ref[0, rho:rho + 1, :], (GRID_W, LANES))
        z = pltpu.roll(z, LANES - (NA_COLS - 1), 1)
        bit = 1
        while bit < GRID_W:
            z = jnp.where((rows & bit) != 0, pltpu.roll(z, bit, 1), z)
            bit *= 2
        toeplitz.append(z)
    kc = lane & (GRID_W - 1)
    first = jnp.clip(rows - NA_COLS // 2, 0, GRID_W - NA_COLS)
    inside = (kc >= first) & (kc < first + NA_COLS)
    for d in range(NA_ROWS):
        for pair in range(NA_ROWS // 2):
            even = toeplitz[d + 2 * pair]
            odd = pltpu.roll(toeplitz[d + 2 * pair + 1], GRID_W, 1)
            tile = jnp.where(lane < GRID_W, even, odd)
            o_ref[d, 0, :, pair * LANES:(pair + 1) * LANES] = jnp.where(inside, tile * LOG2E, MASKED)


def na_bias_table(rpb):
    h = rpb.shape[0]
    padded = jnp.zeros((h, 2 * NA_ROWS, LANES), F32).at[:, :2 * NA_ROWS - 1, :2 * NA_COLS - 1].set(rpb.astype(F32))
    return pl.pallas_call(
        _na_bias_kernel,
        out_shape=jax.ShapeDtypeStruct((NA_ROWS, h, GRID_W, NA_ROWS * GRID_W), F32),
        grid=(h,),
        in_specs=[pl.BlockSpec((1, 2 * NA_ROWS, LANES), lambda i: (i, 0, 0))],
        out_specs=pl.BlockSpec((NA_ROWS, 1, GRID_W, NA_ROWS * GRID_W), lambda i: (0, i, 0, 0)),
        compiler_params=_params("parallel"),
        name="na_bias_table",
    )(padded)


def _neighbourhood_kernel(q_ref, k_ref, v_ref, b_ref, o_ref, *, n_rows, group):
    win = NA_ROWS * GRID_W

    def row(r, carry):
        first = jnp.clip(r - NA_ROWS // 2, 0, n_rows - NA_ROWS)
        q0 = pl.multiple_of(r * GRID_W, GRID_W)
        k0 = pl.multiple_of(first * GRID_W, GRID_W)
        q = _stack_heads(q_ref[0, pl.ds(q0, GRID_W), :], group)
        s = _qk(q, k_ref[0, pl.ds(k0, win), :])
        s = (s.reshape(group, GRID_W, win) + b_ref[first - r + NA_ROWS - 1]).reshape(group * GRID_W, win)
        m = jnp.max(s, axis=-1, keepdims=True)
        o, _ = _normalised_pv(jnp.exp2(s - m).astype(BF16), _with_ones(v_ref[0, pl.ds(k0, win), :]))
        o_ref[0, pl.ds(q0, GRID_W), :] = _unstack_heads(o, group).astype(o_ref.dtype)
        return carry

    lax.fori_loop(0, n_rows, row, 0, unroll=16)


def neighbourhood_attention(proj, bias, *, kv_heads, group):
    b, s, _ = proj.shape
    n_rows = s // GRID_W
    assert n_rows >= NA_ROWS, "the key window is NA_ROWS grid rows tall"
    w = group * HEAD_DIM
    k_block0 = kv_heads * group
    v_block0 = k_block0 + kv_heads
    return pl.pallas_call(
        functools.partial(_neighbourhood_kernel, n_rows=n_rows, group=group),
        out_shape=jax.ShapeDtypeStruct((b, s, kv_heads * w), BF16),
        grid=(b, kv_heads),
        in_specs=[
            pl.BlockSpec((1, s, w), lambda bi, g: (bi, 0, g)),
            pl.BlockSpec((1, s, HEAD_DIM), lambda bi, g: (bi, 0, k_block0 + g)),
            pl.BlockSpec((1, s, HEAD_DIM), lambda bi, g: (bi, 0, v_block0 + g)),
            pl.BlockSpec((NA_ROWS, group, GRID_W, NA_ROWS * GRID_W), lambda bi, g: (0, g, 0, 0)),
        ],
        out_specs=pl.BlockSpec((1, s, w), lambda bi, g: (bi, 0, g)),
        compiler_params=_params("parallel", "arbitrary"),
        name="neighbourhood_attention",
    )(proj, proj, proj, bias)


IN_TILE = 1024


def _with_next_norm(outputs):
    x, xg, sq = outputs
    return x, (xg, (sq, x.shape[1]))


def _out_projection(o, x, w_out, layer, next_gain, name):
    tn = 1024 // len(_row_parts(x))
    return _with_next_norm(matmul(o.reshape(-1, o.shape[-1]), w_out, layer=layer, out_dtype=F32, res=x,
                                  norm_out=next_gain, tk=o.shape[-1], tn=tn, name=name))


def mixer_axial_gqa(normed, x, w_in, w_out, layer, next_gain, q_gain, k_gain, b, s, casts=()):
    segments = [(q_gain * Q_SCALE, A_HEADS), (k_gain, A_KV_HEADS), (None, A_KV_HEADS)]
    tiles = (A_HEADS + 2 * A_KV_HEADS) * HEAD_DIM // IN_TILE
    proj, *cast_weights = _row_parts(in_projection(
        normed, w_in, layer, segments, axial_rope_tables(s), s, tiles,
        (A_HEADS + A_KV_HEADS) * HEAD_DIM // IN_TILE, tn=IN_TILE, name="a_in", casts=casts))
    o = dense_gqa(proj.reshape(b, s, -1), kv_heads=A_KV_HEADS, group=A_HEADS // A_KV_HEADS)
    return (*_out_projection(o, x, w_out, layer, next_gain, "a_out"), *cast_weights)


def mixer_dilated(normed, x, w_in, w_out, layer, next_gain, q_gain, k_gain, b, s):
    segments = []
    for gi in range(len(B_CONFIGS)):
        segments += [(q_gain[gi] * Q_SCALE, B_HEADS), (k_gain[gi], B_KV_HEADS), (None, B_KV_HEADS)]
    tn = B_KV_HEADS * HEAD_DIM
    tiles = (B_HEADS + 2 * B_KV_HEADS) * HEAD_DIM // tn
    proj = in_projection(normed, w_in, layer, segments, full_rope_tables(s), s, tiles, tiles - 1, tn=tn, name="b_in")
    o = dilated_attention(proj.reshape(b, s, -1))
    return _out_projection(o, x, w_out, layer, next_gain, "b_out")


def mixer_differential(normed, x, w_in, w_out, layer, next_gain, q_gain, k_gain, lam_params, subln, lambda_init,
                       b, s):
    segments = [(q_gain * Q_SCALE, 2 * C_HEADS), (k_gain, 2 * C_HEADS), (None, 2 * C_HEADS)]
    tiles = 3 * 2 * C_HEADS * HEAD_DIM // IN_TILE
    proj = in_projection(normed, w_in, layer, segments, full_rope_tables(s), s, tiles, 2 * tiles // 3,
                         tn=IN_TILE, name="c_in")
    o = differential_attention(proj.reshape(b, s, -1), lam_params, subln, lambda_init)
    return _out_projection(o, x, w_out, layer, next_gain, "c_out")


def mixer_neighbourhood(normed, x, w_in, w_out, layer, next_gain, q_gain, k_gain, rpb, b, s):
    segments = [(q_gain * Q_SCALE, D_HEADS), (k_gain, D_KV_HEADS), (None, D_KV_HEADS)]
    tiles = (D_HEADS + 2 * D_KV_HEADS) * HEAD_DIM // IN_TILE
    proj = in_projection(normed, w_in, layer, segments, None, s, tiles,
                         (D_HEADS + D_KV_HEADS) * HEAD_DIM // IN_TILE, tn=IN_TILE, name="d_in")
    o = neighbourhood_attention(proj.reshape(b, s, -1), na_bias_table(rpb),
                                kv_heads=D_KV_HEADS, group=D_HEADS // D_KV_HEADS)
    return _out_projection(o, x, w_out, layer, next_gain, "d_out")


def squared_relu_mlp(normed, x, w_up, w_down, next_gain=None, next_weights=(), split=None):
    a, norm_in = normed
    u, *cast_weights = _row_parts(matmul(a, w_up, out_dtype=BF16, act="relu2", norm_in=norm_in, casts=next_weights,
                                         tk=a.shape[1], name="mlp_up"))
    down = functools.partial(matmul, u, w_down, out_dtype=F32, res=x, tk=4096, name="mlp_down")
    if split is None:
        return (*_with_next_norm(down(norm_out=next_gain)), cast_weights)
    return down(rows=(0, split)), down(rows=(split, u.shape[0] - split))


def trunk(x, b, s, ln_mix, ln_ffn, w_up, w_down, mixers, split):
    depth = ln_mix.shape[0]
    normed = (rmsnorm(x, ln_mix[0]), None)
    mlp_weights = None
    for i in range(depth):
        m, j = i % N_MIXERS, i // N_MIXERS
        p = mixers[m]
        common = (normed, x, p["w_in"], p["w_out"], j, ln_ffn[i],
                  p["q_gain"][j].astype(F32), p["k_gain"][j].astype(F32))
        if m == 0 and i == 0:
            x, normed, mlp_weights = mixer_axial_gqa(*common, b, s, casts=[(w_up, 0), (w_down, 0)])
        elif m == 0:
            x, normed = mixer_axial_gqa(*common, b, s)
        elif m == 1:
            x, normed = mixer_dilated(*common, b, s)
        elif m == 2:
            lambda_init = 0.8 - 0.6 * math.exp(-0.3 * i)
            x, normed = mixer_differential(*common, p["lam"][j], p["subln"][j], lambda_init, b, s)
        else:
            x, normed = mixer_neighbourhood(*common, p["rpb"][j], b, s)
        if mlp_weights is None:
            mlp_weights = [w_up[i].astype(BF16), w_down[i].astype(BF16)]
        if i == depth - 1:
            return squared_relu_mlp(normed, x, *mlp_weights, split=split)
        x, normed, mlp_weights = squared_relu_mlp(normed, x, *mlp_weights, next_gain=ln_mix[i + 1],
                                                  next_weights=[(w_up, i + 1), (w_down, i + 1)])


def kernel(x_prompt, x_sample, ln_mix, ln_ffn, w_up, w_down, a_w_in, a_w_out, a_q_gain, a_k_gain, b_w_in, b_w_out, b_q_gain, b_k_gain, c_w_in, c_w_out, c_q_gain, c_k_gain, c_lambda, c_subln, d_w_in, d_w_out, d_q_gain, d_k_gain, d_rpb):
    bp, s, d = x_prompt.shape
    bs = x_sample.shape[0]
    assert x_sample.shape[1:] == (s, d)
    b = bp + bs
    x = (x_prompt.reshape(bp * s, d), x_sample.reshape(bs * s, d))
    cast = lambda w: w.astype(BF16)
    mixers = [
        dict(w_in=cast(a_w_in), w_out=cast(a_w_out), q_gain=a_q_gain, k_gain=a_k_gain),
        dict(w_in=cast(b_w_in), w_out=cast(b_w_out), q_gain=b_q_gain, k_gain=b_k_gain),
        dict(w_in=cast(c_w_in), w_out=cast(c_w_out), q_gain=c_q_gain, k_gain=c_k_gain,
             lam=c_lambda, subln=c_subln),
        dict(w_in=cast(d_w_in), w_out=cast(d_w_out), q_gain=d_q_gain, k_gain=d_k_gain, rpb=d_rpb),
    ]
    y_prompt, y_sample = trunk(x, b, s, ln_mix, ln_ffn, w_up, w_down, mixers, split=bp * s)
    return (y_prompt.reshape(bp, s, d), y_sample.reshape(bs, s, d))
```

```python
import functools
import math

import jax
import jax.numpy as jnp
from jax import lax
from jax.experimental import pallas as pl
from jax.experimental.pallas import tpu as pltpu

F32 = jnp.float32
BF16 = jnp.bfloat16

HEAD_DIM = 128
GRID_W = 64
ROPE_THETA = 10000.0
EPS = 1e-6
LOG2E = math.log2(math.e)
Q_SCALE = HEAD_DIM ** -0.5 * LOG2E
N_MIXERS = 4
A_HEADS, A_KV_HEADS = 32, 8
B_CONFIGS = ((128, 1), (512, 4), (2048, 16))
B_HEADS, B_KV_HEADS = 16, 4
C_HEADS = 16
D_HEADS, D_KV_HEADS = 32, 8
NA_ROWS, NA_COLS = 8, 16
MASKED = -1e30
RESIDUE_LIFT_BITS = 16

LANES = 128
V7X_VMEM_BYTES = 64 * 1024 * 1024
VMEM_LIMIT_BYTES = V7X_VMEM_BYTES - 8 * 1024 * 1024
MATMUL_TEMP_BYTES = 3 * 1024 * 1024
CAST_ROWS, CAST_COLS = 256, 1024


def _tile(dim, pref, quantum):
    if dim <= pref:
        return dim
    t = (pref // quantum) * quantum
    while dim % t:
        t -= quantum
    return t


def _params(*semantics):
    return pltpu.CompilerParams(dimension_semantics=semantics, vmem_limit_bytes=VMEM_LIMIT_BYTES)


def _row_parts(x):
    return x if isinstance(x, tuple) else (x,)


def _row_part_specs(parts, tile_rows, width, row_tile, col_tile):
    specs, first = [], 0
    for p in parts:
        n = p.shape[0] // tile_rows
        specs.append(pl.BlockSpec(
            (tile_rows, width),
            lambda *g, first=first, n=n: (jnp.clip(row_tile(*g) - first, 0, n - 1), col_tile(*g))))
        first += n
    return specs


def _read_part(refs, part_tiles, row_tile, read):
    value, first = None, 0
    for ref, n in zip(refs, part_tiles):
        value = read(ref) if value is None else jnp.where(row_tile >= first, read(ref), value)
        first += n
    return value


def _rmsnorm_kernel(*refs, part_tiles):
    *x_refs, g_ref, o_ref = refs

    def emit(x_ref):
        x = x_ref[...]
        ms = jnp.mean(x * x, axis=-1, keepdims=True)
        o_ref[...] = (x * lax.rsqrt(ms + EPS) * g_ref[...]).astype(o_ref.dtype)

    first = 0
    for x_ref, n in zip(x_refs, part_tiles):
        i = pl.program_id(0)
        pl.when((i >= first) & (i < first + n))(functools.partial(emit, x_ref))
        first += n


def rmsnorm(x, gain):
    parts = _row_parts(x)
    d = parts[0].shape[1]
    t = sum(p.shape[0] for p in parts)
    tr = _tile(math.gcd(*[p.shape[0] for p in parts]), 512, 8)
    return pl.pallas_call(
        functools.partial(_rmsnorm_kernel, part_tiles=[p.shape[0] // tr for p in parts]),
        out_shape=jax.ShapeDtypeStruct((t, d), BF16),
        grid=(t // tr,),
        in_specs=_row_part_specs(parts, tr, d, lambda i: i, lambda i: 0) + [pl.BlockSpec((1, d), lambda i: (0, 0))],
        out_specs=pl.BlockSpec((tr, d), lambda i: (i, 0)),
        compiler_params=_params("parallel"),
        name="rmsnorm",
    )(*parts, gain.reshape(1, d).astype(F32))


def _lane_partial_sums(v):
    return functools.reduce(lambda a, b: a + b, [v[:, c:c + LANES] for c in range(0, v.shape[1], LANES)])


def _matmul_kernel(a_ref, w_ref, *refs, nk, act, res_tiles, first_tile, chunk, norm_in, norm_out, norm_dim,
                   n_casts):
    refs = list(refs)
    n_out = (3 if norm_out else 1) + n_casts
    outs, refs = refs[len(refs) - n_out:], refs[:len(refs) - n_out]
    o_ref, norm_out_refs, cast_out_refs = outs[0], outs[1:n_out - n_casts], outs[n_out - n_casts:]
    cast_in_refs = [refs.pop() for _ in range(n_casts)][::-1]
    gain_ref = refs.pop() if norm_out else None
    sq_ref = refs.pop() if norm_in else None
    res_refs = refs
    k = pl.program_id(2)

    for src_ref, dst_ref in zip(cast_in_refs, cast_out_refs):
        dst_ref[...] = src_ref[...].astype(dst_ref.dtype)

    def residual(cols):
        return _read_part(res_refs, res_tiles, first_tile + pl.program_id(0), lambda ref: ref[:, cols])

    def emit_norm(x, cols, first):
        xg_ref, sq_out_ref = norm_out_refs
        xg_ref[:, cols] = (x * gain_ref[:, cols]).astype(xg_ref.dtype)
        sq = _lane_partial_sums(x * x)
        sq_out_ref[...] = sq if first else sq_out_ref[...] + sq

    if norm_in:
        scale = lax.rsqrt(jnp.sum(sq_ref[...], axis=-1, keepdims=True) * (1.0 / norm_dim) + EPS)

    if nk > 1:
        @pl.when(k == 0)
        def _():
            o_ref[...] = residual(slice(None)) if res_refs else jnp.zeros_like(o_ref)

    for c in range(o_ref.shape[1] // chunk):
        cols = slice(c * chunk, (c + 1) * chunk)
        part = jnp.dot(a_ref[...], w_ref[:, cols], preferred_element_type=F32)
        if nk > 1:
            part = o_ref[:, cols] + part
            o_ref[:, cols] = part
            if norm_out:
                emit_norm(part, cols, c == 0)
            continue
        if norm_in:
            part = part * scale
        if res_refs:
            part = part + residual(cols)
        if act == "relu2":
            r = jnp.maximum(part, 0.0)
            part = r * r
        o_ref[:, cols] = part.astype(o_ref.dtype)
        if norm_out:
            emit_norm(part, cols, c == 0)


def matmul(a, w, *, out_dtype, layer=None, act=None, res=None, rows=None, norm_in=None, norm_out=None,
           casts=(), tm=1024, tn=1024, tk=2048, chunk=256, name="matmul"):
    row0, m = rows if rows is not None else (0, a.shape[0])
    kdim, n = w.shape[-2:]
    res_parts = _row_parts(res) if res is not None else ()
    tm = _tile(math.gcd(m, row0, *[p.shape[0] for p in res_parts[:-1]]), tm, 8)
    tn, tk = _tile(n, tn, LANES), _tile(kdim, tk, LANES)
    cast_rows_of = []
    for src, _ in casts:
        cast_rows = CAST_ROWS
        while (src.shape[-2] // cast_rows) * (src.shape[-1] // CAST_COLS) > (m // tm) * (n // tn):
            cast_rows *= 2
        cast_rows_of.append(cast_rows)

    def window_bytes(tk):
        per_step = tm * tk * 2 + tk * tn * 2 + tm * tn * (4 * len(res_parts) + jnp.dtype(out_dtype).itemsize)
        if norm_out is not None:
            per_step += tm * tn * 2 + tm * LANES * 4
        if norm_in is not None:
            per_step += tm * norm_in[0].shape[1] * 4
        per_step += sum(cast_rows_of) * CAST_COLS * (4 + 2)
        return 2 * per_step

    while window_bytes(tk) > VMEM_LIMIT_BYTES - MATMUL_TEMP_BYTES and tk % (2 * LANES) == 0:
        tk //= 2
    nk = kdim // tk
    t0 = row0 // tm
    chunk = min(chunk, tn)
    assert nk == 1 or (out_dtype == F32 and act is None), "a split K accumulates in the f32 output block"
    if w.ndim == 3:
        w_spec = pl.BlockSpec((None, tk, tn), lambda i, j, k: (layer, k, j))
    else:
        w_spec = pl.BlockSpec((tk, tn), lambda i, j, k: (k, j))
    in_specs = [pl.BlockSpec((tm, tk), lambda i, j, k: (t0 + i, k)), w_spec]
    args = [a, w]
    in_specs += _row_part_specs(res_parts, tm, tn, lambda i, j, k: t0 + i, lambda i, j, k: j)
    args += res_parts
    norm_dim = None
    if norm_in is not None:
        sq, norm_dim = norm_in
        assert nk == 1, "the row scale is applied where the whole product is formed"
        in_specs.append(pl.BlockSpec((tm, sq.shape[1]), lambda i, j, k: (t0 + i, 0)))
        args.append(sq)
    out_shape = jax.ShapeDtypeStruct((m, n), out_dtype)
    out_specs = pl.BlockSpec((tm, tn), lambda i, j, k: (i, j))
    if norm_out is not None:
        assert out_dtype == F32
        in_specs.append(pl.BlockSpec((1, tn), lambda i, j, k: (0, j)))
        args.append(norm_out.reshape(1, n).astype(F32))
        out_shape = (out_shape, jax.ShapeDtypeStruct((m, n), BF16),
                     jax.ShapeDtypeStruct((m, n // tn * LANES), F32))
        out_specs = (out_specs, pl.BlockSpec((tm, tn), lambda i, j, k: (i, j)),
                     pl.BlockSpec((tm, LANES), lambda i, j, k: (i, j)))
    grid = (m // tm, n // tn, nk)
    for (src, src_layer), cast_rows in zip(casts, cast_rows_of):
        rows_, cols_ = src.shape[-2] // cast_rows, src.shape[-1] // CAST_COLS

        def block(i, j, k, cols_=cols_, last=rows_ * cols_ - 1):
            step = jnp.minimum((i * grid[1] + j) * grid[2] + k, last)
            return step // cols_, step % cols_

        in_specs.append(pl.BlockSpec((None, cast_rows, CAST_COLS),
                                     lambda i, j, k, block=block, src_layer=src_layer: (src_layer, *block(i, j, k))))
        args.append(src)
        out_shape = (*_row_parts(out_shape), jax.ShapeDtypeStruct(src.shape[-2:], BF16))
        out_specs = (*_row_parts(out_specs), pl.BlockSpec((cast_rows, CAST_COLS), block))
    return pl.pallas_call(
        functools.partial(_matmul_kernel, nk=nk, act=act, chunk=chunk, first_tile=t0,
                          res_tiles=[p.shape[0] // tm for p in res_parts], n_casts=len(casts),
                          norm_in=norm_in is not None, norm_out=norm_out is not None, norm_dim=norm_dim),
        out_shape=out_shape,
        grid=grid,
        in_specs=in_specs,
        out_specs=out_specs,
        compiler_params=_params("parallel", "parallel", "arbitrary"),
        name=name,
    )(*args)


def rope_tables(pos, dim):
    inv = ROPE_THETA ** (-jnp.arange(0, dim, 2, dtype=F32) / dim)
    ang = pos.astype(F32)[:, None] * inv[None, :]
    ang = jnp.concatenate([ang, ang], axis=-1)
    return jnp.cos(ang), jnp.sin(ang)


def full_rope_tables(s):
    cos, sin = rope_tables(jnp.arange(s), HEAD_DIM)
    lane = jnp.arange(HEAD_DIM)
    sin_signed = jnp.where(lane < HEAD_DIM // 2, -sin, sin)
    return ((HEAD_DIM // 2,), (cos, sin_signed))


def axial_rope_tables(s):
    half = HEAD_DIM // 2
    t = jnp.arange(s)
    cr, sr = rope_tables(t // GRID_W, half)
    cc, sc = rope_tables(t % GRID_W, half)
    cos = jnp.concatenate([cr, cc], axis=-1)
    sin = jnp.concatenate([sr, sc], axis=-1)
    low = (jnp.arange(HEAD_DIM) % half) < half // 2
    return ((HEAD_DIM - half // 2, half // 2),
            (cos, jnp.where(low, -sin, 0.0), jnp.where(low, 0.0, sin)))


def _prep_kernel(x_ref, gain_ref, *refs, shifts):
    *tab_refs, o_ref = refs
    tables = [t[...] for t in tab_refs]
    averager = jnp.full((HEAD_DIM, HEAD_DIM), 1.0 / HEAD_DIM, BF16)
    for h0 in range(0, x_ref.shape[1], HEAD_DIM):
        head = slice(h0, h0 + HEAD_DIM)
        x = x_ref[:, head].astype(F32)
        ms = jnp.dot((x * x).astype(BF16), averager, preferred_element_type=F32)
        y = x * lax.rsqrt(ms + EPS) * gain_ref[:, head]
        if shifts:
            r = y * tables[0]
            for sh, t in zip(shifts, tables[1:]):
                r = r + pltpu.roll(y, sh, 1) * t
            y = r
        o_ref[:, head] = y.astype(o_ref.dtype)


def in_projection(normed, w_in, layer, segments, rope, seq, tiles_per_group, v_tile, *, tn, name, casts=()):
    gains = jnp.concatenate([jnp.tile(jnp.ones((HEAD_DIM,), F32) if g is None else g.astype(F32), n)
                             for g, n in segments])[None]
    shifts, tables = rope if rope is not None else ((), ())
    a, norm_in = normed
    proj, *cast_weights = _row_parts(matmul(a, w_in, layer=layer, out_dtype=BF16, norm_in=norm_in, casts=casts,
                                            tk=a.shape[1], name=name))
    t, n = proj.shape
    ts = _tile(seq, 1024, 16)
    per_seq = seq // ts
    qk_tiles = n // tn // tiles_per_group * v_tile
    tile = lambda j: j // v_tile * tiles_per_group + j % v_tile
    prepped = pl.pallas_call(
        functools.partial(_prep_kernel, shifts=shifts),
        out_shape=jax.ShapeDtypeStruct((t, n), BF16),
        grid=(t // ts, qk_tiles),
        in_specs=[pl.BlockSpec((ts, tn), lambda i, j: (i, tile(j))), pl.BlockSpec((1, tn), lambda i, j: (0, tile(j)))]
        + [pl.BlockSpec((ts, HEAD_DIM), lambda i, j: (i % per_seq, 0))] * len(tables),
        out_specs=pl.BlockSpec((ts, tn), lambda i, j: (i, tile(j))),
        input_output_aliases={0: 0},
        compiler_params=_params("parallel", "parallel"),
        name=name + "_prep",
    )(proj, gains, *[tab.astype(F32) for tab in tables])
    return (prepped, cast_weights) if casts else prepped


def _stack_heads(x, n):
    return jnp.concatenate([x[:, h * HEAD_DIM:(h + 1) * HEAD_DIM] for h in range(n)], axis=0)


def _unstack_heads(x, n):
    rows = x.shape[0] // n
    return jnp.concatenate([x[h * rows:(h + 1) * rows] for h in range(n)], axis=1)


def _qk(q, k):
    return lax.dot_general(q, k, (((1,), (1,)), ((), ())), preferred_element_type=F32)


def _with_ones(v):
    return jnp.concatenate([v, jnp.ones_like(v)], axis=1)


def _normalised_pv(p, v_ones):
    r = jnp.dot(p, v_ones, preferred_element_type=F32)
    l = r[:, HEAD_DIM:]
    return r[:, :HEAD_DIM] / l, l


def _dense_gqa_kernel(q_ref, k_ref, v_ref, o_ref, *, group, chains):
    k, v = k_ref[0], _with_ones(v_ref[0])
    per = group // chains

    def scores(c):
        return _qk(_stack_heads(q_ref[0, :, c * per * HEAD_DIM:(c + 1) * per * HEAD_DIM], per), k)

    ahead = [scores(c) for c in range(min(2, chains))]
    for c in range(chains):
        cols = slice(c * per * HEAD_DIM, (c + 1) * per * HEAD_DIM)
        s = ahead.pop(0)
        if c + 2 < chains:
            ahead.append(scores(c + 2))
        m = jnp.max(s, axis=-1, keepdims=True)
        o, _ = _normalised_pv(jnp.exp2(s - m).astype(BF16), v)
        o_ref[0, :, cols] = _unstack_heads(o, per).astype(o_ref.dtype)


def dense_gqa(proj, *, kv_heads, group, tq=256, chains=4):
    b, s, _ = proj.shape
    tq = _tile(s, tq, 16)
    w = group * HEAD_DIM
    k_block0 = kv_heads * group
    v_block0 = k_block0 + kv_heads
    return pl.pallas_call(
        functools.partial(_dense_gqa_kernel, group=group, chains=chains),
        out_shape=jax.ShapeDtypeStruct((b, s, kv_heads * w), BF16),
        grid=(b, kv_heads, s // tq),
        in_specs=[
            pl.BlockSpec((1, tq, w), lambda bi, g, i: (bi, i, g)),
            pl.BlockSpec((1, s, HEAD_DIM), lambda bi, g, i: (bi, 0, k_block0 + g)),
            pl.BlockSpec((1, s, HEAD_DIM), lambda bi, g, i: (bi, 0, v_block0 + g)),
        ],
        out_specs=pl.BlockSpec((1, tq, w), lambda bi, g, i: (bi, i, g)),
        compiler_params=_params("parallel", "parallel", "arbitrary"),
        name="dense_gqa",
    )(proj, proj, proj)


def _dilated_kernel(*refs, configs, seq, group):
    o_ref = refs[-1]
    i = pl.program_id(2)
    tq = o_ref.shape[1]

    chains = []
    for gi, (window, dil) in enumerate(configs):
        reach = (window // (2 * dil)) * dil
        pad = -(-reach // tq) * tq
        wlen = min(seq, tq + 2 * pad)
        start = pl.multiple_of(jnp.clip(i * tq - pad, 0, seq - wlen), tq)
        u = (start + lax.broadcasted_iota(jnp.int32, (tq, wlen), 1)
             - i * tq - lax.broadcasted_iota(jnp.int32, (tq, wlen), 0) + reach)
        if dil > 1:
            u = u + ((u & (dil - 1)) << RESIDUE_LIFT_BITS)
        ok = lax.bitcast_convert_type(u, jnp.uint32) <= jnp.uint32(2 * reach)
        bias = jnp.where(ok, 0.0, MASKED)
        per = group if wlen <= 4 * tq else max(1, group // 4)
        for h0 in range(0, group, per):
            chains.append((gi, h0, per, start, wlen, bias))

    def scores(c):
        gi, h0, per, start, wlen, _ = chains[c]
        q_ref, k_ref = refs[3 * gi], refs[3 * gi + 1]
        q = _stack_heads(q_ref[0, :, h0 * HEAD_DIM:(h0 + per) * HEAD_DIM], per)
        return _qk(q, k_ref[0, pl.ds(start, wlen), :])

    outs = [[None] * group for _ in configs]
    lses = [[None] * group for _ in configs]
    ahead = [scores(c) for c in range(min(2, len(chains)))]
    for c, (gi, h0, per, start, wlen, bias) in enumerate(chains):
        s = ahead.pop(0)
        if c + 2 < len(chains):
            ahead.append(scores(c + 2))
        s = (s.reshape(per, tq, wlen) + bias[None]).reshape(per * tq, wlen)
        m = jnp.max(s, axis=-1, keepdims=True)
        v_ref = refs[3 * gi + 2]
        o, l = _normalised_pv(jnp.exp2(s - m).astype(BF16), _with_ones(v_ref[0, pl.ds(start, wlen), :]))
        lse = m + jnp.log2(l[:, :1])
        for h in range(per):
            outs[gi][h0 + h] = o[h * tq:(h + 1) * tq]
            lses[gi][h0 + h] = lse[h * tq:(h + 1) * tq]

    for h in range(group):
        lse_h = [lses[gi][h] for gi in range(len(configs))]
        top = functools.reduce(jnp.maximum, lse_h)
        ws = [jnp.exp2(x - top) for x in lse_h]
        o = sum(w * outs[gi][h] for gi, w in enumerate(ws)) / sum(ws)
        o_ref[0, :, h * HEAD_DIM:(h + 1) * HEAD_DIM] = o.astype(o_ref.dtype)


def dilated_attention(proj, *, configs=B_CONFIGS, tq=256):
    b, s, _ = proj.shape
    tq = _tile(s, tq, LANES)
    for window, dil in configs:
        assert dil & (dil - 1) == 0, "the residue test uses a power-of-two dilation"
        assert 2 * window + s < 1 << RESIDUE_LIFT_BITS, "a lifted residue must clear every in-range offset"
    group = B_HEADS // B_KV_HEADS
    w = group * HEAD_DIM
    per_group_blocks = B_HEADS + 2 * B_KV_HEADS
    in_specs, args = [], []
    for gi in range(len(configs)):
        q0 = gi * per_group_blocks // group
        k0 = gi * per_group_blocks + B_HEADS
        in_specs += [
            pl.BlockSpec((1, tq, w), lambda bi, h, i, q0=q0: (bi, i, q0 + h)),
            pl.BlockSpec((1, s, HEAD_DIM), lambda bi, h, i, k0=k0: (bi, 0, k0 + h)),
            pl.BlockSpec((1, s, HEAD_DIM), lambda bi, h, i, k0=k0: (bi, 0, k0 + B_KV_HEADS + h)),
        ]
        args += [proj, proj, proj]
    return pl.pallas_call(
        functools.partial(_dilated_kernel, configs=configs, seq=s, group=group),
        out_shape=jax.ShapeDtypeStruct((b, s, B_HEADS * HEAD_DIM), BF16),
        grid=(b, B_KV_HEADS, s // tq),
        in_specs=in_specs,
        out_specs=pl.BlockSpec((1, tq, w), lambda bi, h, i: (bi, i, h)),
        compiler_params=_params("parallel", "parallel", "arbitrary"),
        name="dilated_attention",
    )(*args)


def _differential_kernel(q_ref, k_ref, v_ref, lam_ref, sub_ref, o_ref, *, lambda_init, chains):
    lp = lam_ref[...]
    lam = (jnp.exp(jnp.sum(lp[0:1] * lp[1:2], axis=-1, keepdims=True))
           - jnp.exp(jnp.sum(lp[2:3] * lp[3:4], axis=-1, keepdims=True)) + lambda_init)
    k, v = k_ref[0], v_ref[0]
    rows = q_ref.shape[1] // chains
    def scores(c):
        q = q_ref[0, c * rows:(c + 1) * rows, :]
        return [_qk(q[:, mi * HEAD_DIM:(mi + 1) * HEAD_DIM], k[:, mi * HEAD_DIM:(mi + 1) * HEAD_DIM])
                for mi in range(2)]

    nxt = scores(0)
    for c in range(chains):
        cur = nxt
        if c + 1 < chains:
            nxt = scores(c + 1)
        probs = []
        for s in cur:
            m = jnp.max(s, axis=-1, keepdims=True)
            p = jnp.exp2(s - m)
            probs.append((p, jnp.sum(p, axis=-1, keepdims=True)))
        (p0, l0), (p1, l1) = probs
        a = p0 - p1 * (lam * l0 / l1)
        o = jnp.dot(a.astype(BF16), v, preferred_element_type=F32) / l0
        ms = jnp.mean(o * o, axis=-1, keepdims=True)
        o = o * lax.rsqrt(ms + EPS) * sub_ref[...] * (1.0 - lambda_init)
        o_ref[0, c * rows:(c + 1) * rows, :] = o.astype(o_ref.dtype)


def differential_attention(proj, lam_params, subln, lambda_init, *, tq=1024, chains=4):
    b, s, _ = proj.shape
    tq = _tile(s, tq, 16)
    w = 2 * HEAD_DIM
    return pl.pallas_call(
        functools.partial(_differential_kernel, lambda_init=lambda_init, chains=chains),
        out_shape=jax.ShapeDtypeStruct((b, s, C_HEADS * w), BF16),
        grid=(b, C_HEADS, s // tq),
        in_specs=[
            pl.BlockSpec((1, tq, w), lambda bi, h, i: (bi, i, h)),
            pl.BlockSpec((1, s, w), lambda bi, h, i: (bi, 0, C_HEADS + h)),
            pl.BlockSpec((1, s, w), lambda bi, h, i: (bi, 0, 2 * C_HEADS + h)),
            pl.BlockSpec((4, HEAD_DIM), lambda bi, h, i: (0, 0)),
            pl.BlockSpec((1, w), lambda bi, h, i: (0, 0)),
        ],
        out_specs=pl.BlockSpec((1, tq, w), lambda bi, h, i: (bi, i, h)),
        compiler_params=_params("parallel", "parallel", "arbitrary"),
        name="differential_attention",
    )(proj, proj, proj, lam_params.astype(F32), subln.reshape(1, w).astype(F32))


def _na_bias_kernel(rpb_ref, o_ref):
    rows = lax.broadcasted_iota(jnp.int32, (GRID_W, LANES), 0)
    lane = lax.broadcasted_iota(jnp.int32, (GRID_W, LANES), 1)
    n_rel = 2 * NA_ROWS - 1
    toeplitz = []
    for rho in range(n_rel):
        z = jnp.broadcast_to(rpb_ref[0, rho:rho + 1, :], (GRID_W, LANES))
        z = pltpu.roll(z, LANES - (NA_COLS - 1), 1)
        bit = 1
        while bit < GRID_W:
            z = jnp.where((rows & bit) != 0, pltpu.roll(z, bit, 1), z)
            bit *= 2
        toeplitz.append(z)
    kc = lane & (GRID_W - 1)
    first = jnp.clip(rows - NA_COLS // 2, 0, GRID_W - NA_COLS)
    inside = (kc >= first) & (kc < first + NA_COLS)
    for d in range(NA_ROWS):
        for pair in range(NA_ROWS // 2):
            even = toeplitz[d + 2 * pair]
            odd = pltpu.roll(toeplitz[d + 2 * pair + 1], GRID_W, 1)
            tile = jnp.where(lane < GRID_W, even, odd)
            o_ref[d, 0, :, pair * LANES:(pair + 1) * LANES] = jnp.where(inside, tile * LOG2E, MASKED)


def na_bias_table(rpb):
    h = rpb.shape[0]
    padded = jnp.zeros((h, 2 * NA_ROWS, LANES), F32).at[:, :2 * NA_ROWS - 1, :2 * NA_COLS - 1].set(rpb.astype(F32))
    return pl.pallas_call(
        _na_bias_kernel,
        out_shape=jax.ShapeDtypeStruct((NA_ROWS, h, GRID_W, NA_ROWS * GRID_W), F32),
        grid=(h,),
        in_specs=[pl.BlockSpec((1, 2 * NA_ROWS, LANES), lambda i: (i, 0, 0))],
        out_specs=pl.BlockSpec((NA_ROWS, 1, GRID_W, NA_ROWS * GRID_W), lambda i: (0, i, 0, 0)),
        compiler_params=_params("parallel"),
        name="na_bias_table",
    )(padded)


def _neighbourhood_kernel(q_ref, k_ref, v_ref, b_ref, o_ref, *, n_rows, group):
    win = NA_ROWS * GRID_W

    def row(r, carry):
        first = jnp.clip(r - NA_ROWS // 2, 0, n_rows - NA_ROWS)
        q0 = pl.multiple_of(r * GRID_W, GRID_W)
        k0 = pl.multiple_of(first * GRID_W, GRID_W)
        q = _stack_heads(q_ref[0, pl.ds(q0, GRID_W), :], group)
        s = _qk(q, k_ref[0, pl.ds(k0, win), :])
        s = (s.reshape(group, GRID_W, win) + b_ref[first - r + NA_ROWS - 1]).reshape(group * GRID_W, win)
        m = jnp.max(s, axis=-1, keepdims=True)
        o, _ = _normalised_pv(jnp.exp2(s - m).astype(BF16), _with_ones(v_ref[0, pl.ds(k0, win), :]))
        o_ref[0, pl.ds(q0, GRID_W), :] = _unstack_heads(o, group).astype(o_ref.dtype)
        return carry

    lax.fori_loop(0, n_rows, row, 0, unroll=16)


def neighbourhood_attention(proj, bias, *, kv_heads, group):
    b, s, _ = proj.shape
    n_rows = s // GRID_W
    assert n_rows >= NA_ROWS, "the key window is NA_ROWS grid rows tall"
    w = group * HEAD_DIM
    k_block0 = kv_heads * group
    v_block0 = k_block0 + kv_heads
    return pl.pallas_call(
        functools.partial(_neighbourhood_kernel, n_rows=n_rows, group=group),
        out_shape=jax.ShapeDtypeStruct((b, s, kv_heads * w), BF16),
        grid=(b, kv_heads),
        in_specs=[
            pl.BlockSpec((1, s, w), lambda bi, g: (bi, 0, g)),
            pl.BlockSpec((1, s, HEAD_DIM), lambda bi, g: (bi, 0, k_block0 + g)),
            pl.BlockSpec((1, s, HEAD_DIM), lambda bi, g: (bi, 0, v_block0 + g)),
            pl.BlockSpec((NA_ROWS, group, GRID_W, NA_ROWS * GRID_W), lambda bi, g: (0, g, 0, 0)),
        ],
        out_specs=pl.BlockSpec((1, s, w), lambda bi, g: (bi, 0, g)),
        compiler_params=_params("parallel", "arbitrary"),
        name="neighbourhood_attention",
    )(proj, proj, proj, bias)


IN_TILE = 1024


def _with_next_norm(outputs):
    x, xg, sq = outputs
    return x, (xg, (sq, x.shape[1]))


def _out_projection(o, x, w_out, layer, next_gain, name):
    tn = 1024 // len(_row_parts(x))
    return _with_next_norm(matmul(o.reshape(-1, o.shape[-1]), w_out, layer=layer, out_dtype=F32, res=x,
                                  norm_out=next_gain, tk=o.shape[-1], tn=tn, name=name))


def mixer_axial_gqa(normed, x, w_in, w_out, layer, next_gain, q_gain, k_gain, b, s, casts=()):
    segments = [(q_gain * Q_SCALE, A_HEADS), (k_gain, A_KV_HEADS), (None, A_KV_HEADS)]
    tiles = (A_HEADS + 2 * A_KV_HEADS) * HEAD_DIM // IN_TILE
    proj, *cast_weights = _row_parts(in_projection(
        normed, w_in, layer, segments, axial_rope_tables(s), s, tiles,
        (A_HEADS + A_KV_HEADS) * HEAD_DIM // IN_TILE, tn=IN_TILE, name="a_in", casts=casts))
    o = dense_gqa(proj.reshape(b, s, -1), kv_heads=A_KV_HEADS, group=A_HEADS // A_KV_HEADS)
    return (*_out_projection(o, x, w_out, layer, next_gain, "a_out"), *cast_weights)


def mixer_dilated(normed, x, w_in, w_out, layer, next_gain, q_gain, k_gain, b, s):
    segments = []
    for gi in range(len(B_CONFIGS)):
        segments += [(q_gain[gi] * Q_SCALE, B_HEADS), (k_gain[gi], B_KV_HEADS), (None, B_KV_HEADS)]
    tn = B_KV_HEADS * HEAD_DIM
    tiles = (B_HEADS + 2 * B_KV_HEADS) * HEAD_DIM // tn
    proj = in_projection(normed, w_in, layer, segments, full_rope_tables(s), s, tiles, tiles - 1, tn=tn, name="b_in")
    o = dilated_attention(proj.reshape(b, s, -1))
    return _out_projection(o, x, w_out, layer, next_gain, "b_out")


def mixer_differential(normed, x, w_in, w_out, layer, next_gain, q_gain, k_gain, lam_params, subln, lambda_init,
                       b, s):
    segments = [(q_gain * Q_SCALE, 2 * C_HEADS), (k_gain, 2 * C_HEADS), (None, 2 * C_HEADS)]
    tiles = 3 * 2 * C_HEADS * HEAD_DIM // IN_TILE
    proj = in_projection(normed, w_in, layer, segments, full_rope_tables(s), s, tiles, 2 * tiles // 3,
                         tn=IN_TILE, name="c_in")
    o = differential_attention(proj.reshape(b, s, -1), lam_params, subln, lambda_init)
    return _out_projection(o, x, w_out, layer, next_gain, "c_out")


def mixer_neighbourhood(normed, x, w_in, w_out, layer, next_gain, q_gain, k_gain, rpb, b, s):
    segments = [(q_gain * Q_SCALE, D_HEADS), (k_gain, D_KV_HEADS), (None, D_KV_HEADS)]
    tiles = (D_HEADS + 2 * D_KV_HEADS) * HEAD_DIM // IN_TILE
    proj = in_projection(normed, w_in, layer, segments, None, s, tiles,
                         (D_HEADS + D_KV_HEADS) * HEAD_DIM // IN_TILE, tn=IN_TILE, name="d_in")
    o = neighbourhood_attention(proj.reshape(b, s, -1), na_bias_table(rpb),
                                kv_heads=D_KV_HEADS, group=D_HEADS // D_KV_HEADS)
    return _out_projection(o, x, w_out, layer, next_gain, "d_out")


def squared_relu_mlp(normed, x, w_up, w_down, next_gain=None, next_weights=(), split=None):
    a, norm_in = normed
    u, *cast_weights = _row_parts(matmul(a, w_up, out_dtype=BF16, act="relu2", norm_in=norm_in, casts=next_weights,
                                         tk=a.shape[1], name="mlp_up"))
    down = functools.partial(matmul, u, w_down, out_dtype=F32, res=x, tk=4096, name="mlp_down")
    if split is None:
        return (*_with_next_norm(down(norm_out=next_gain)), cast_weights)
    return down(rows=(0, split)), down(rows=(split, u.shape[0] - split))


def trunk(x, b, s, ln_mix, ln_ffn, w_up, w_down, mixers, split):
    depth = ln_mix.shape[0]
    normed = (rmsnorm(x, ln_mix[0]), None)
    mlp_weights = None
    mixer_weights = (mixers[0]["w_in"].astype(BF16), mixers[0]["w_out"].astype(BF16))
    for i in range(depth):
        m, j = i % N_MIXERS, i // N_MIXERS
        p = mixers[m]
        layer = j if mixer_weights[0].ndim == 3 else None
        common = (normed, x, *mixer_weights, layer, ln_ffn[i],
                  p["q_gain"][j].astype(F32), p["k_gain"][j].astype(F32))
        if m == 0 and i == 0:
            x, normed, mlp_weights = mixer_axial_gqa(*common, b, s, casts=[(w_up, 0), (w_down, 0)])
        elif m == 0:
            x, normed = mixer_axial_gqa(*common, b, s)
        elif m == 1:
            x, normed = mixer_dilated(*common, b, s)
        elif m == 2:
            lambda_init = 0.8 - 0.6 * math.exp(-0.3 * i)
            x, normed = mixer_differential(*common, p["lam"][j], p["subln"][j], lambda_init, b, s)
        else:
            x, normed = mixer_neighbourhood(*common, p["rpb"][j], b, s)
        if mlp_weights is None:
            mlp_weights = [w_up[i].astype(BF16), w_down[i].astype(BF16)]
        if i == depth - 1:
            return squared_relu_mlp(normed, x, *mlp_weights, split=split)
        nxt, nxt_j = mixers[(i + 1) % N_MIXERS], (i + 1) // N_MIXERS
        next_weights = [(w_up, i + 1), (w_down, i + 1)]
        if i > 0:
            next_weights += [(nxt["w_in"], nxt_j), (nxt["w_out"], nxt_j)]
        x, normed, cast_weights = squared_relu_mlp(normed, x, *mlp_weights, next_gain=ln_mix[i + 1],
                                                  next_weights=next_weights)
        mlp_weights, mixer_weights = cast_weights[:2], tuple(cast_weights[2:])
        if not mixer_weights:
            mixer_weights = (nxt["w_in"][nxt_j].astype(BF16), nxt["w_out"][nxt_j].astype(BF16))


def kernel(x_prompt, x_sample, ln_mix, ln_ffn, w_up, w_down, a_w_in, a_w_out, a_q_gain, a_k_gain, b_w_in, b_w_out, b_q_gain, b_k_gain, c_w_in, c_w_out, c_q_gain, c_k_gain, c_lambda, c_subln, d_w_in, d_w_out, d_q_gain, d_k_gain, d_rpb):
    bp, s, d = x_prompt.shape
    bs = x_sample.shape[0]
    assert x_sample.shape[1:] == (s, d)
    b = bp + bs
    x = (x_prompt.reshape(bp * s, d), x_sample.reshape(bs * s, d))
    mixers = [
        dict(w_in=a_w_in, w_out=a_w_out, q_gain=a_q_gain, k_gain=a_k_gain),
        dict(w_in=b_w_in, w_out=b_w_out, q_gain=b_q_gain, k_gain=b_k_gain),
        dict(w_in=c_w_in, w_out=c_w_out, q_gain=c_q_gain, k_gain=c_k_gain, lam=c_lambda, subln=c_subln),
        dict(w_in=d_w_in, w_out=d_w_out, q_gain=d_q_gain, k_gain=d_k_gain, rpb=d_rpb),
    ]
    y_prompt, y_sample = trunk(x, b, s, ln_mix, ln_ffn, w_up, w_down, mixers, split=bp * s)
    return (y_prompt.reshape(bp, s, d), y_sample.reshape(bs, s, d))
```

```python
import functools
import math

import jax
import jax.numpy as jnp
from jax import lax
from jax.experimental import pallas as pl
from jax.experimental.pallas import tpu as pltpu

F32 = jnp.float32
BF16 = jnp.bfloat16

HEAD_DIM = 128
GRID_W = 64
ROPE_THETA = 10000.0
EPS = 1e-6
LOG2E = math.log2(math.e)
Q_SCALE = HEAD_DIM ** -0.5 * LOG2E
N_MIXERS = 4
A_HEADS, A_KV_HEADS = 32, 8
B_CONFIGS = ((128, 1), (512, 4), (2048, 16))
B_HEADS, B_KV_HEADS = 16, 4
C_HEADS = 16
D_HEADS, D_KV_HEADS = 32, 8
NA_ROWS, NA_COLS = 8, 16
MASKED = -1e30
RESIDUE_LIFT_BITS = 16

LANES = 128
V7X_VMEM_BYTES = 64 * 1024 * 1024
VMEM_LIMIT_BYTES = V7X_VMEM_BYTES - 8 * 1024 * 1024
MATMUL_TEMP_BYTES = 3 * 1024 * 1024
CAST_ROWS, CAST_COLS = 256, 1024


def _tile(dim, pref, quantum):
    if dim <= pref:
        return dim
    t = (pref // quantum) * quantum
    while dim % t:
        t -= quantum
    return t


def _params(*semantics):
    return pltpu.CompilerParams(dimension_semantics=semantics, vmem_limit_bytes=VMEM_LIMIT_BYTES)


def _row_parts(x):
    return x if isinstance(x, tuple) else (x,)


def _row_part_specs(parts, tile_rows, width, row_tile, col_tile):
    specs, first = [], 0
    for p in parts:
        n = p.shape[0] // tile_rows
        specs.append(pl.BlockSpec(
            (tile_rows, width),
            lambda *g, first=first, n=n: (jnp.clip(row_tile(*g) - first, 0, n - 1), col_tile(*g))))
        first += n
    return specs


def _read_part(refs, part_tiles, row_tile, read):
    value, first = None, 0
    for ref, n in zip(refs, part_tiles):
        value = read(ref) if value is None else jnp.where(row_tile >= first, read(ref), value)
        first += n
    return value


def _rmsnorm_kernel(*refs, part_tiles):
    *x_refs, g_ref, o_ref = refs

    def emit(x_ref):
        x = x_ref[...]
        ms = jnp.mean(x * x, axis=-1, keepdims=True)
        o_ref[...] = (x * lax.rsqrt(ms + EPS) * g_ref[...]).astype(o_ref.dtype)

    first = 0
    for x_ref, n in zip(x_refs, part_tiles):
        i = pl.program_id(0)
        pl.when((i >= first) & (i < first + n))(functools.partial(emit, x_ref))
        first += n


def rmsnorm(x, gain):
    parts = _row_parts(x)
    d = parts[0].shape[1]
    t = sum(p.shape[0] for p in parts)
    tr = _tile(math.gcd(*[p.shape[0] for p in parts]), 512, 8)
    return pl.pallas_call(
        functools.partial(_rmsnorm_kernel, part_tiles=[p.shape[0] // tr for p in parts]),
        out_shape=jax.ShapeDtypeStruct((t, d), BF16),
        grid=(t // tr,),
        in_specs=_row_part_specs(parts, tr, d, lambda i: i, lambda i: 0) + [pl.BlockSpec((1, d), lambda i: (0, 0))],
        out_specs=pl.BlockSpec((tr, d), lambda i: (i, 0)),
        compiler_params=_params("parallel"),
        name="rmsnorm",
    )(*parts, gain.reshape(1, d).astype(F32))


def _lane_partial_sums(v):
    return functools.reduce(lambda a, b: a + b, [v[:, c:c + LANES] for c in range(0, v.shape[1], LANES)])


def _matmul_kernel(a_ref, w_ref, *refs, nk, act, res_tiles, first_tile, chunk, norm_in, norm_out, norm_dim,
                   n_casts):
    refs = list(refs)
    n_out = (3 if norm_out else 1) + n_casts
    outs, refs = refs[len(refs) - n_out:], refs[:len(refs) - n_out]
    o_ref, norm_out_refs, cast_out_refs = outs[0], outs[1:n_out - n_casts], outs[n_out - n_casts:]
    cast_in_refs = [refs.pop() for _ in range(n_casts)][::-1]
    gain_ref = refs.pop() if norm_out else None
    sq_ref = refs.pop() if norm_in else None
    res_refs = refs
    k = pl.program_id(2)

    for src_ref, dst_ref in zip(cast_in_refs, cast_out_refs):
        dst_ref[...] = src_ref[...].astype(dst_ref.dtype)

    def residual(cols):
        return _read_part(res_refs, res_tiles, first_tile + pl.program_id(0), lambda ref: ref[:, cols])

    def emit_norm(x, cols, first):
        xg_ref, sq_out_ref = norm_out_refs
        xg_ref[:, cols] = (x * gain_ref[:, cols]).astype(xg_ref.dtype)
        sq = _lane_partial_sums(x * x)
        sq_out_ref[...] = sq if first else sq_out_ref[...] + sq

    if norm_in:
        scale = lax.rsqrt(jnp.sum(sq_ref[...], axis=-1, keepdims=True) * (1.0 / norm_dim) + EPS)

    if nk > 1:
        @pl.when(k == 0)
        def _():
            o_ref[...] = residual(slice(None)) if res_refs else jnp.zeros_like(o_ref)

    for c in range(o_ref.shape[1] // chunk):
        cols = slice(c * chunk, (c + 1) * chunk)
        part = jnp.dot(a_ref[...], w_ref[:, cols], preferred_element_type=F32)
        if nk > 1:
            part = o_ref[:, cols] + part
            o_ref[:, cols] = part
            if norm_out:
                emit_norm(part, cols, c == 0)
            continue
        if norm_in:
            part = part * scale
        if res_refs:
            part = part + residual(cols)
        if act == "relu2":
            r = jnp.maximum(part, 0.0)
            part = r * r
        o_ref[:, cols] = part.astype(o_ref.dtype)
        if norm_out:
            emit_norm(part, cols, c == 0)


def matmul(a, w, *, out_dtype, layer=None, act=None, res=None, rows=None, norm_in=None, norm_out=None,
           casts=(), tm=1024, tn=1024, tk=2048, chunk=256, name="matmul"):
    row0, m = rows if rows is not None else (0, a.shape[0])
    kdim, n = w.shape[-2:]
    res_parts = _row_parts(res) if res is not None else ()
    tm = _tile(math.gcd(m, row0, *[p.shape[0] for p in res_parts[:-1]]), tm, 8)
    tn, tk = _tile(n, tn, LANES), _tile(kdim, tk, LANES)
    cast_rows_of = []
    for src, _ in casts:
        cast_rows = CAST_ROWS
        while (src.shape[-2] // cast_rows) * (src.shape[-1] // CAST_COLS) > (m // tm) * (n // tn):
            cast_rows *= 2
        cast_rows_of.append(cast_rows)

    def window_bytes(tk):
        per_step = tm * tk * 2 + tk * tn * 2 + tm * tn * (4 * len(res_parts) + jnp.dtype(out_dtype).itemsize)
        if norm_out is not None:
            per_step += tm * tn * 2 + tm * LANES * 4
        if norm_in is not None:
            per_step += tm * norm_in[0].shape[1] * 4
        per_step += sum(cast_rows_of) * CAST_COLS * (4 + 2)
        return 2 * per_step

    while window_bytes(tk) > VMEM_LIMIT_BYTES - MATMUL_TEMP_BYTES and tk % (2 * LANES) == 0:
        tk //= 2
    nk = kdim // tk
    t0 = row0 // tm
    chunk = min(chunk, tn)
    assert nk == 1 or (out_dtype == F32 and act is None), "a split K accumulates in the f32 output block"
    if w.ndim == 3:
        w_spec = pl.BlockSpec((None, tk, tn), lambda i, j, k: (layer, k, j))
    else:
        w_spec = pl.BlockSpec((tk, tn), lambda i, j, k: (k, j))
    in_specs = [pl.BlockSpec((tm, tk), lambda i, j, k: (t0 + i, k)), w_spec]
    args = [a, w]
    in_specs += _row_part_specs(res_parts, tm, tn, lambda i, j, k: t0 + i, lambda i, j, k: j)
    args += res_parts
    norm_dim = None
    if norm_in is not None:
        sq, norm_dim = norm_in
        assert nk == 1, "the row scale is applied where the whole product is formed"
        in_specs.append(pl.BlockSpec((tm, sq.shape[1]), lambda i, j, k: (t0 + i, 0)))
        args.append(sq)
    out_shape = jax.ShapeDtypeStruct((m, n), out_dtype)
    out_specs = pl.BlockSpec((tm, tn), lambda i, j, k: (i, j))
    if norm_out is not None:
        assert out_dtype == F32
        in_specs.append(pl.BlockSpec((1, tn), lambda i, j, k: (0, j)))
        args.append(norm_out.reshape(1, n).astype(F32))
        out_shape = (out_shape, jax.ShapeDtypeStruct((m, n), BF16),
                     jax.ShapeDtypeStruct((m, n // tn * LANES), F32))
        out_specs = (out_specs, pl.BlockSpec((tm, tn), lambda i, j, k: (i, j)),
                     pl.BlockSpec((tm, LANES), lambda i, j, k: (i, j)))
    grid = (m // tm, n // tn, nk)
    for (src, src_layer), cast_rows in zip(casts, cast_rows_of):
        rows_, cols_ = src.shape[-2] // cast_rows, src.shape[-1] // CAST_COLS

        def block(i, j, k, cols_=cols_, last=rows_ * cols_ - 1):
            step = jnp.minimum((i * grid[1] + j) * grid[2] + k, last)
            return step // cols_, step % cols_

        in_specs.append(pl.BlockSpec((None, cast_rows, CAST_COLS),
                                     lambda i, j, k, block=block, src_layer=src_layer: (src_layer, *block(i, j, k))))
        args.append(src)
        out_shape = (*_row_parts(out_shape), jax.ShapeDtypeStruct(src.shape[-2:], BF16))
        out_specs = (*_row_parts(out_specs), pl.BlockSpec((cast_rows, CAST_COLS), block))
    return pl.pallas_call(
        functools.partial(_matmul_kernel, nk=nk, act=act, chunk=chunk, first_tile=t0,
                          res_tiles=[p.shape[0] // tm for p in res_parts], n_casts=len(casts),
                          norm_in=norm_in is not None, norm_out=norm_out is not None, norm_dim=norm_dim),
        out_shape=out_shape,
        grid=grid,
        in_specs=in_specs,
        out_specs=out_specs,
        compiler_params=_params("parallel", "parallel", "arbitrary"),
        name=name,
    )(*args)


def rope_tables(pos, dim):
    inv = ROPE_THETA ** (-jnp.arange(0, dim, 2, dtype=F32) / dim)
    ang = pos.astype(F32)[:, None] * inv[None, :]
    ang = jnp.concatenate([ang, ang], axis=-1)
    return jnp.cos(ang), jnp.sin(ang)


def full_rope_tables(s):
    cos, sin = rope_tables(jnp.arange(s), HEAD_DIM)
    lane = jnp.arange(HEAD_DIM)
    sin_signed = jnp.where(lane < HEAD_DIM // 2, -sin, sin)
    return ((HEAD_DIM // 2,), (cos, sin_signed))


def axial_rope_tables(s):
    half = HEAD_DIM // 2
    t = jnp.arange(s)
    cr, sr = rope_tables(t // GRID_W, half)
    cc, sc = rope_tables(t % GRID_W, half)
    cos = jnp.concatenate([cr, cc], axis=-1)
    sin = jnp.concatenate([sr, sc], axis=-1)
    low = (jnp.arange(HEAD_DIM) % half) < half // 2
    return ((HEAD_DIM - half // 2, half // 2),
            (cos, jnp.where(low, -sin, 0.0), jnp.where(low, 0.0, sin)))


def _prep_kernel(x_ref, gain_ref, *refs, shifts):
    *tab_refs, o_ref = refs
    tables = [t[...] for t in tab_refs]
    averager = jnp.full((HEAD_DIM, HEAD_DIM), 1.0 / HEAD_DIM, BF16)
    for h0 in range(0, x_ref.shape[1], HEAD_DIM):
        head = slice(h0, h0 + HEAD_DIM)
        x = x_ref[:, head].astype(F32)
        ms = jnp.dot((x * x).astype(BF16), averager, preferred_element_type=F32)
        y = x * lax.rsqrt(ms + EPS) * gain_ref[:, head]
        if shifts:
            r = y * tables[0]
            for sh, t in zip(shifts, tables[1:]):
                r = r + pltpu.roll(y, sh, 1) * t
            y = r
        o_ref[:, head] = y.astype(o_ref.dtype)


def in_projection(normed, w_in, layer, segments, rope, seq, tiles_per_group, v_tile, *, tn, name, casts=()):
    gains = jnp.concatenate([jnp.tile(jnp.ones((HEAD_DIM,), F32) if g is None else g.astype(F32), n)
                             for g, n in segments])[None]
    shifts, tables = rope if rope is not None else ((), ())
    a, norm_in = normed
    proj, *cast_weights = _row_parts(matmul(a, w_in, layer=layer, out_dtype=BF16, norm_in=norm_in, casts=casts,
                                            tk=a.shape[1], name=name))
    t, n = proj.shape
    ts = _tile(seq, 1024, 16)
    per_seq = seq // ts
    qk_tiles = n // tn // tiles_per_group * v_tile
    tile = lambda j: j // v_tile * tiles_per_group + j % v_tile
    prepped = pl.pallas_call(
        functools.partial(_prep_kernel, shifts=shifts),
        out_shape=jax.ShapeDtypeStruct((t, n), BF16),
        grid=(t // ts, qk_tiles),
        in_specs=[pl.BlockSpec((ts, tn), lambda i, j: (i, tile(j))), pl.BlockSpec((1, tn), lambda i, j: (0, tile(j)))]
        + [pl.BlockSpec((ts, HEAD_DIM), lambda i, j: (i % per_seq, 0))] * len(tables),
        out_specs=pl.BlockSpec((ts, tn), lambda i, j: (i, tile(j))),
        input_output_aliases={0: 0},
        compiler_params=_params("parallel", "parallel"),
        name=name + "_prep",
    )(proj, gains, *[tab.astype(F32) for tab in tables])
    return (prepped, cast_weights) if casts else prepped


def _stack_heads(x, n):
    return jnp.concatenate([x[:, h * HEAD_DIM:(h + 1) * HEAD_DIM] for h in range(n)], axis=0)


def _unstack_heads(x, n):
    rows = x.shape[0] // n
    return jnp.concatenate([x[h * rows:(h + 1) * rows] for h in range(n)], axis=1)


def _qk(q, k):
    return lax.dot_general(q, k, (((1,), (1,)), ((), ())), preferred_element_type=F32)


def _with_ones(v):
    return jnp.concatenate([v, jnp.ones_like(v)], axis=1)


def _normalised_pv(p, v_ones):
    r = jnp.dot(p, v_ones, preferred_element_type=F32)
    l = r[:, HEAD_DIM:]
    return r[:, :HEAD_DIM] / l, l


def _dense_gqa_kernel(q_ref, k_ref, v_ref, o_ref, *, group, chains):
    k, v = k_ref[0], _with_ones(v_ref[0])
    per = group // chains

    def scores(c):
        return _qk(_stack_heads(q_ref[0, :, c * per * HEAD_DIM:(c + 1) * per * HEAD_DIM], per), k)

    ahead = [scores(c) for c in range(min(2, chains))]
    for c in range(chains):
        cols = slice(c * per * HEAD_DIM, (c + 1) * per * HEAD_DIM)
        s = ahead.pop(0)
        if c + 2 < chains:
            ahead.append(scores(c + 2))
        m = jnp.max(s, axis=-1, keepdims=True)
        o, _ = _normalised_pv(jnp.exp2(s - m).astype(BF16), v)
        o_ref[0, :, cols] = _unstack_heads(o, per).astype(o_ref.dtype)


def dense_gqa(proj, *, kv_heads, group, tq=512, chains=4):
    b, s, _ = proj.shape
    tq = _tile(s, tq, 16)
    w = group * HEAD_DIM
    k_block0 = kv_heads * group
    v_block0 = k_block0 + kv_heads
    return pl.pallas_call(
        functools.partial(_dense_gqa_kernel, group=group, chains=chains),
        out_shape=jax.ShapeDtypeStruct((b, s, kv_heads * w), BF16),
        grid=(b, kv_heads, s // tq),
        in_specs=[
            pl.BlockSpec((1, tq, w), lambda bi, g, i: (bi, i, g)),
            pl.BlockSpec((1, s, HEAD_DIM), lambda bi, g, i: (bi, 0, k_block0 + g)),
            pl.BlockSpec((1, s, HEAD_DIM), lambda bi, g, i: (bi, 0, v_block0 + g)),
        ],
        out_specs=pl.BlockSpec((1, tq, w), lambda bi, g, i: (bi, i, g)),
        compiler_params=_params("parallel", "parallel", "arbitrary"),
        name="dense_gqa",
    )(proj, proj, proj)


def _dilated_kernel(*refs, configs, seq, group):
    o_ref = refs[-1]
    i = pl.program_id(2)
    tq = o_ref.shape[1]

    chains = []
    for gi, (window, dil) in enumerate(configs):
        reach = (window // (2 * dil)) * dil
        pad = -(-reach // tq) * tq
        wlen = min(seq, tq + 2 * pad)
        start = pl.multiple_of(jnp.clip(i * tq - pad, 0, seq - wlen), tq)
        u = (start + lax.broadcasted_iota(jnp.int32, (tq, wlen), 1)
             - i * tq - lax.broadcasted_iota(jnp.int32, (tq, wlen), 0) + reach)
        if dil > 1:
            u = u + ((u & (dil - 1)) << RESIDUE_LIFT_BITS)
        ok = lax.bitcast_convert_type(u, jnp.uint32) <= jnp.uint32(2 * reach)
        bias = jnp.where(ok, 0.0, MASKED)
        per = group if wlen <= 4 * tq else max(1, group // 4)
        for h0 in range(0, group, per):
            chains.append((gi, h0, per, start, wlen, bias))

    def scores(c):
        gi, h0, per, start, wlen, _ = chains[c]
        q_ref, k_ref = refs[3 * gi], refs[3 * gi + 1]
        q = _stack_heads(q_ref[0, :, h0 * HEAD_DIM:(h0 + per) * HEAD_DIM], per)
        return _qk(q, k_ref[0, pl.ds(start, wlen), :])

    outs = [[None] * group for _ in configs]
    lses = [[None] * group for _ in configs]
    ahead = [scores(c) for c in range(min(2, len(chains)))]
    for c, (gi, h0, per, start, wlen, bias) in enumerate(chains):
        s = ahead.pop(0)
        if c + 2 < len(chains):
            ahead.append(scores(c + 2))
        s = (s.reshape(per, tq, wlen) + bias[None]).reshape(per * tq, wlen)
        m = jnp.max(s, axis=-1, keepdims=True)
        v_ref = refs[3 * gi + 2]
        o, l = _normalised_pv(jnp.exp2(s - m).astype(BF16), _with_ones(v_ref[0, pl.ds(start, wlen), :]))
        lse = m + jnp.log2(l[:, :1])
        for h in range(per):
            outs[gi][h0 + h] = o[h * tq:(h + 1) * tq]
            lses[gi][h0 + h] = lse[h * tq:(h + 1) * tq]

    for h in range(group):
        lse_h = [lses[gi][h] for gi in range(len(configs))]
        top = functools.reduce(jnp.maximum, lse_h)
        ws = [jnp.exp2(x - top) for x in lse_h]
        o = sum(w * outs[gi][h] for gi, w in enumerate(ws)) / sum(ws)
        o_ref[0, :, h * HEAD_DIM:(h + 1) * HEAD_DIM] = o.astype(o_ref.dtype)


def dilated_attention(proj, *, configs=B_CONFIGS, tq=256):
    b, s, _ = proj.shape
    tq = _tile(s, tq, LANES)
    for window, dil in configs:
        assert dil & (dil - 1) == 0, "the residue test uses a power-of-two dilation"
        assert 2 * window + s < 1 << RESIDUE_LIFT_BITS, "a lifted residue must clear every in-range offset"
    group = B_HEADS // B_KV_HEADS
    w = group * HEAD_DIM
    per_group_blocks = B_HEADS + 2 * B_KV_HEADS
    in_specs, args = [], []
    for gi in range(len(configs)):
        q0 = gi * per_group_blocks // group
        k0 = gi * per_group_blocks + B_HEADS
        in_specs += [
            pl.BlockSpec((1, tq, w), lambda bi, h, i, q0=q0: (bi, i, q0 + h)),
            pl.BlockSpec((1, s, HEAD_DIM), lambda bi, h, i, k0=k0: (bi, 0, k0 + h)),
            pl.BlockSpec((1, s, HEAD_DIM), lambda bi, h, i, k0=k0: (bi, 0, k0 + B_KV_HEADS + h)),
        ]
        args += [proj, proj, proj]
    return pl.pallas_call(
        functools.partial(_dilated_kernel, configs=configs, seq=s, group=group),
        out_shape=jax.ShapeDtypeStruct((b, s, B_HEADS * HEAD_DIM), BF16),
        grid=(b, B_KV_HEADS, s // tq),
        in_specs=in_specs,
        out_specs=pl.BlockSpec((1, tq, w), lambda bi, h, i: (bi, i, h)),
        compiler_params=_params("parallel", "parallel", "arbitrary"),
        name="dilated_attention",
    )(*args)


def _differential_kernel(q_ref, k_ref, v_ref, lam_ref, sub_ref, o_ref, *, lambda_init, chains):
    lp = lam_ref[...]
    lam = (jnp.exp(jnp.sum(lp[0:1] * lp[1:2], axis=-1, keepdims=True))
           - jnp.exp(jnp.sum(lp[2:3] * lp[3:4], axis=-1, keepdims=True)) + lambda_init)
    k, v = k_ref[0], v_ref[0]
    rows = q_ref.shape[1] // chains
    def scores(c):
        q = q_ref[0, c * rows:(c + 1) * rows, :]
        return [_qk(q[:, mi * HEAD_DIM:(mi + 1) * HEAD_DIM], k[:, mi * HEAD_DIM:(mi + 1) * HEAD_DIM])
                for mi in range(2)]

    nxt = scores(0)
    for c in range(chains):
        cur = nxt
        if c + 1 < chains:
            nxt = scores(c + 1)
        probs = []
        for s in cur:
            m = jnp.max(s, axis=-1, keepdims=True)
            p = jnp.exp2(s - m)
            probs.append((p, jnp.sum(p, axis=-1, keepdims=True)))
        (p0, l0), (p1, l1) = probs
        a = p0 - p1 * (lam * l0 / l1)
        o = jnp.dot(a.astype(BF16), v, preferred_element_type=F32) / l0
        ms = jnp.mean(o * o, axis=-1, keepdims=True)
        o = o * lax.rsqrt(ms + EPS) * sub_ref[...] * (1.0 - lambda_init)
        o_ref[0, c * rows:(c + 1) * rows, :] = o.astype(o_ref.dtype)


def differential_attention(proj, lam_params, subln, lambda_init, *, tq=2048, chains=8):
    b, s, _ = proj.shape
    tq = _tile(s, tq, 16)
    w = 2 * HEAD_DIM
    return pl.pallas_call(
        functools.partial(_differential_kernel, lambda_init=lambda_init, chains=chains),
        out_shape=jax.ShapeDtypeStruct((b, s, C_HEADS * w), BF16),
        grid=(b, C_HEADS, s // tq),
        in_specs=[
            pl.BlockSpec((1, tq, w), lambda bi, h, i: (bi, i, h)),
            pl.BlockSpec((1, s, w), lambda bi, h, i: (bi, 0, C_HEADS + h)),
            pl.BlockSpec((1, s, w), lambda bi, h, i: (bi, 0, 2 * C_HEADS + h)),
            pl.BlockSpec((4, HEAD_DIM), lambda bi, h, i: (0, 0)),
            pl.BlockSpec((1, w), lambda bi, h, i: (0, 0)),
        ],
        out_specs=pl.BlockSpec((1, tq, w), lambda bi, h, i: (bi, i, h)),
        compiler_params=_params("parallel", "parallel", "arbitrary"),
        name="differential_attention",
    )(proj, proj, proj, lam_params.astype(F32), subln.reshape(1, w).astype(F32))


def _na_bias_kernel(rpb_ref, o_ref):
    rows = lax.broadcasted_iota(jnp.int32, (GRID_W, LANES), 0)
    lane = lax.broadcasted_iota(jnp.int32, (GRID_W, LANES), 1)
    n_rel = 2 * NA_ROWS - 1
    toeplitz = []
    for rho in range(n_rel):
        z = jnp.broadcast_to(rpb_ref[0, rho:rho + 1, :], (GRID_W, LANES))
        z = pltpu.roll(z, LANES - (NA_COLS - 1), 1)
        bit = 1
        while bit < GRID_W:
            z = jnp.where((rows & bit) != 0, pltpu.roll(z, bit, 1), z)
            bit *= 2
        toeplitz.append(z)
    kc = lane & (GRID_W - 1)
    first = jnp.clip(rows - NA_COLS // 2, 0, GRID_W - NA_COLS)
    inside = (kc >= first) & (kc < first + NA_COLS)
    for d in range(NA_ROWS):
        for pair in range(NA_ROWS // 2):
            even = toeplitz[d + 2 * pair]
            odd = pltpu.roll(toeplitz[d + 2 * pair + 1], GRID_W, 1)
            tile = jnp.where(lane < GRID_W, even, odd)
            o_ref[d, 0, :, pair * LANES:(pair + 1) * LANES] = jnp.where(inside, tile * LOG2E, MASKED)


def na_bias_table(rpb):
    h = rpb.shape[0]
    padded = jnp.zeros((h, 2 * NA_ROWS, LANES), F32).at[:, :2 * NA_ROWS - 1, :2 * NA_COLS - 1].set(rpb.astype(F32))
    return pl.pallas_call(
        _na_bias_kernel,
        out_shape=jax.ShapeDtypeStruct((NA_ROWS, h, GRID_W, NA_ROWS * GRID_W), F32),
        grid=(h,),
        in_specs=[pl.BlockSpec((1, 2 * NA_ROWS, LANES), lambda i: (i, 0, 0))],
        out_specs=pl.BlockSpec((NA_ROWS, 1, GRID_W, NA_ROWS * GRID_W), lambda i: (0, i, 0, 0)),
        compiler_params=_params("parallel"),
        name="na_bias_table",
    )(padded)


def _neighbourhood_kernel(q_ref, k_ref, v_ref, b_ref, o_ref, *, n_rows, group):
    win = NA_ROWS * GRID_W

    def row(r, carry):
        first = jnp.clip(r - NA_ROWS // 2, 0, n_rows - NA_ROWS)
        q0 = pl.multiple_of(r * GRID_W, GRID_W)
        k0 = pl.multiple_of(first * GRID_W, GRID_W)
        q = _stack_heads(q_ref[0, pl.ds(q0, GRID_W), :], group)
        s = _qk(q, k_ref[0, pl.ds(k0, win), :])
        s = (s.reshape(group, GRID_W, win) + b_ref[first - r + NA_ROWS - 1]).reshape(group * GRID_W, win)
        m = jnp.max(s, axis=-1, keepdims=True)
        o, _ = _normalised_pv(jnp.exp2(s - m).astype(BF16), _with_ones(v_ref[0, pl.ds(k0, win), :]))
        o_ref[0, pl.ds(q0, GRID_W), :] = _unstack_heads(o, group).astype(o_ref.dtype)
        return carry

    lax.fori_loop(0, n_rows, row, 0, unroll=16)


def neighbourhood_attention(proj, bias, *, kv_heads, group):
    b, s, _ = proj.shape
    n_rows = s // GRID_W
    assert n_rows >= NA_ROWS, "the key window is NA_ROWS grid rows tall"
    w = group * HEAD_DIM
    k_block0 = kv_heads * group
    v_block0 = k_block0 + kv_heads
    return pl.pallas_call(
        functools.partial(_neighbourhood_kernel, n_rows=n_rows, group=group),
        out_shape=jax.ShapeDtypeStruct((b, s, kv_heads * w), BF16),
        grid=(b, kv_heads),
        in_specs=[
            pl.BlockSpec((1, s, w), lambda bi, g: (bi, 0, g)),
            pl.BlockSpec((1, s, HEAD_DIM), lambda bi, g: (bi, 0, k_block0 + g)),
            pl.BlockSpec((1, s, HEAD_DIM), lambda bi, g: (bi, 0, v_block0 + g)),
            pl.BlockSpec((NA_ROWS, group, GRID_W, NA_ROWS * GRID_W), lambda bi, g: (0, g, 0, 0)),
        ],
        out_specs=pl.BlockSpec((1, s, w), lambda bi, g: (bi, 0, g)),
        compiler_params=_params("parallel", "arbitrary"),
        name="neighbourhood_attention",
    )(proj, proj, proj, bias)


IN_TILE = 1024


def _with_next_norm(outputs):
    x, xg, sq = outputs
    return x, (xg, (sq, x.shape[1]))


def _out_projection(o, x, w_out, layer, next_gain, name):
    tn = 1024 // len(_row_parts(x))
    return _with_next_norm(matmul(o.reshape(-1, o.shape[-1]), w_out, layer=layer, out_dtype=F32, res=x,
                                  norm_out=next_gain, tk=o.shape[-1], tn=tn, name=name))


def mixer_axial_gqa(normed, x, w_in, w_out, layer, next_gain, q_gain, k_gain, b, s, casts=()):
    segments = [(q_gain * Q_SCALE, A_HEADS), (k_gain, A_KV_HEADS), (None, A_KV_HEADS)]
    tiles = (A_HEADS + 2 * A_KV_HEADS) * HEAD_DIM // IN_TILE
    proj, *cast_weights = _row_parts(in_projection(
        normed, w_in, layer, segments, axial_rope_tables(s), s, tiles,
        (A_HEADS + A_KV_HEADS) * HEAD_DIM // IN_TILE, tn=IN_TILE, name="a_in", casts=casts))
    o = dense_gqa(proj.reshape(b, s, -1), kv_heads=A_KV_HEADS, group=A_HEADS // A_KV_HEADS)
    return (*_out_projection(o, x, w_out, layer, next_gain, "a_out"), *cast_weights)


def mixer_dilated(normed, x, w_in, w_out, layer, next_gain, q_gain, k_gain, b, s):
    segments = []
    for gi in range(len(B_CONFIGS)):
        segments += [(q_gain[gi] * Q_SCALE, B_HEADS), (k_gain[gi], B_KV_HEADS), (None, B_KV_HEADS)]
    tn = B_KV_HEADS * HEAD_DIM
    tiles = (B_HEADS + 2 * B_KV_HEADS) * HEAD_DIM // tn
    proj = in_projection(normed, w_in, layer, segments, full_rope_tables(s), s, tiles, tiles - 1, tn=tn, name="b_in")
    o = dilated_attention(proj.reshape(b, s, -1))
    return _out_projection(o, x, w_out, layer, next_gain, "b_out")


def mixer_differential(normed, x, w_in, w_out, layer, next_gain, q_gain, k_gain, lam_params, subln, lambda_init,
                       b, s):
    segments = [(q_gain * Q_SCALE, 2 * C_HEADS), (k_gain, 2 * C_HEADS), (None, 2 * C_HEADS)]
    tiles = 3 * 2 * C_HEADS * HEAD_DIM // IN_TILE
    proj = in_projection(normed, w_in, layer, segments, full_rope_tables(s), s, tiles, 2 * tiles // 3,
                         tn=IN_TILE, name="c_in")
    o = differential_attention(proj.reshape(b, s, -1), lam_params, subln, lambda_init)
    return _out_projection(o, x, w_out, layer, next_gain, "c_out")


def mixer_neighbourhood(normed, x, w_in, w_out, layer, next_gain, q_gain, k_gain, rpb, b, s):
    segments = [(q_gain * Q_SCALE, D_HEADS), (k_gain, D_KV_HEADS), (None, D_KV_HEADS)]
    tiles = (D_HEADS + 2 * D_KV_HEADS) * HEAD_DIM // IN_TILE
    proj = in_projection(normed, w_in, layer, segments, None, s, tiles,
                         (D_HEADS + D_KV_HEADS) * HEAD_DIM // IN_TILE, tn=IN_TILE, name="d_in")
    o = neighbourhood_attention(proj.reshape(b, s, -1), na_bias_table(rpb),
                                kv_heads=D_KV_HEADS, group=D_HEADS // D_KV_HEADS)
    return _out_projection(o, x, w_out, layer, next_gain, "d_out")


def squared_relu_mlp(normed, x, w_up, w_down, next_gain=None, next_weights=(), split=None):
    a, norm_in = normed
    u, *cast_weights = _row_parts(matmul(a, w_up, out_dtype=BF16, act="relu2", norm_in=norm_in, casts=next_weights,
                                         tk=a.shape[1], name="mlp_up"))
    down = functools.partial(matmul, u, w_down, out_dtype=F32, res=x, tk=4096, name="mlp_down")
    if split is None:
        return (*_with_next_norm(down(norm_out=next_gain)), cast_weights)
    return down(rows=(0, split)), down(rows=(split, u.shape[0] - split))


def trunk(x, b, s, ln_mix, ln_ffn, w_up, w_down, mixers, split):
    depth = ln_mix.shape[0]
    normed = (rmsnorm(x, ln_mix[0]), None)
    mlp_weights = None
    mixer_weights = (mixers[0]["w_in"].astype(BF16), mixers[0]["w_out"].astype(BF16))
    for i in range(depth):
        m, j = i % N_MIXERS, i // N_MIXERS
        p = mixers[m]
        layer = j if mixer_weights[0].ndim == 3 else None
        common = (normed, x, *mixer_weights, layer, ln_ffn[i],
                  p["q_gain"][j].astype(F32), p["k_gain"][j].astype(F32))
        if m == 0 and i == 0:
            x, normed, mlp_weights = mixer_axial_gqa(*common, b, s, casts=[(w_up, 0), (w_down, 0)])
        elif m == 0:
            x, normed = mixer_axial_gqa(*common, b, s)
        elif m == 1:
            x, normed = mixer_dilated(*common, b, s)
        elif m == 2:
            lambda_init = 0.8 - 0.6 * math.exp(-0.3 * i)
            x, normed = mixer_differential(*common, p["lam"][j], p["subln"][j], lambda_init, b, s)
        else:
            x, normed = mixer_neighbourhood(*common, p["rpb"][j], b, s)
        if mlp_weights is None:
            mlp_weights = [w_up[i].astype(BF16), w_down[i].astype(BF16)]
        if i == depth - 1:
            return squared_relu_mlp(normed, x, *mlp_weights, split=split)
        nxt, nxt_j = mixers[(i + 1) % N_MIXERS], (i + 1) // N_MIXERS
        next_weights = [(w_up, i + 1), (w_down, i + 1)]
        if i > 0:
            next_weights += [(nxt["w_in"], nxt_j), (nxt["w_out"], nxt_j)]
        x, normed, cast_weights = squared_relu_mlp(normed, x, *mlp_weights, next_gain=ln_mix[i + 1],
                                                  next_weights=next_weights)
        mlp_weights, mixer_weights = cast_weights[:2], tuple(cast_weights[2:])
        if not mixer_weights:
            mixer_weights = (nxt["w_in"][nxt_j].astype(BF16), nxt["w_out"][nxt_j].astype(BF16))


def kernel(x_prompt, x_sample, ln_mix, ln_ffn, w_up, w_down, a_w_in, a_w_out, a_q_gain, a_k_gain, b_w_in, b_w_out, b_q_gain, b_k_gain, c_w_in, c_w_out, c_q_gain, c_k_gain, c_lambda, c_subln, d_w_in, d_w_out, d_q_gain, d_k_gain, d_rpb):
    bp, s, d = x_prompt.shape
    bs = x_sample.shape[0]
    assert x_sample.shape[1:] == (s, d)
    b = bp + bs
    x = (x_prompt.reshape(bp * s, d), x_sample.reshape(bs * s, d))
    mixers = [
        dict(w_in=a_w_in, w_out=a_w_out, q_gain=a_q_gain, k_gain=a_k_gain),
        dict(w_in=b_w_in, w_out=b_w_out, q_gain=b_q_gain, k_gain=b_k_gain),
        dict(w_in=c_w_in, w_out=c_w_out, q_gain=c_q_gain, k_gain=c_k_gain, lam=c_lambda, subln=c_subln),
        dict(w_in=d_w_in, w_out=d_w_out, q_gain=d_q_gain, k_gain=d_k_gain, rpb=d_rpb),
    ]
    y_prompt, y_sample = trunk(x, b, s, ln_mix, ln_ffn, w_up, w_down, mixers, split=bp * s)
    return (y_prompt.reshape(bp, s, d), y_sample.reshape(bs, s, d))
```

```python
import functools
import math

import jax
import jax.numpy as jnp
from jax import lax
from jax.experimental import pallas as pl
from jax.experimental.pallas import tpu as pltpu

F32 = jnp.float32
BF16 = jnp.bfloat16

HEAD_DIM = 128
GRID_W = 64
ROPE_THETA = 10000.0
EPS = 1e-6
LOG2E = math.log2(math.e)
Q_SCALE = HEAD_DIM ** -0.5 * LOG2E
N_MIXERS = 4
A_HEADS, A_KV_HEADS = 32, 8
B_CONFIGS = ((128, 1), (512, 4), (2048, 16))
B_HEADS, B_KV_HEADS = 16, 4
C_HEADS = 16
D_HEADS, D_KV_HEADS = 32, 8
NA_ROWS, NA_COLS = 8, 16
MASKED = -1e30
RESIDUE_LIFT_BITS = 16

LANES = 128
V7X_VMEM_BYTES = 64 * 1024 * 1024
VMEM_LIMIT_BYTES = V7X_VMEM_BYTES - 8 * 1024 * 1024
MATMUL_TEMP_BYTES = 3 * 1024 * 1024
CAST_ROWS, CAST_COLS = 256, 1024


def _tile(dim, pref, quantum):
    if dim <= pref:
        return dim
    t = (pref // quantum) * quantum
    while dim % t:
        t -= quantum
    return t


def _params(*semantics):
    return pltpu.CompilerParams(dimension_semantics=semantics, vmem_limit_bytes=VMEM_LIMIT_BYTES)


def _row_parts(x):
    return x if isinstance(x, tuple) else (x,)


def _row_part_specs(parts, tile_rows, width, row_tile, col_tile):
    specs, first = [], 0
    for p in parts:
        n = p.shape[0] // tile_rows
        specs.append(pl.BlockSpec(
            (tile_rows, width),
            lambda *g, first=first, n=n: (jnp.clip(row_tile(*g) - first, 0, n - 1), col_tile(*g))))
        first += n
    return specs


def _read_part(refs, part_tiles, row_tile, read):
    value, first = None, 0
    for ref, n in zip(refs, part_tiles):
        value = read(ref) if value is None else jnp.where(row_tile >= first, read(ref), value)
        first += n
    return value


def _rmsnorm_kernel(*refs, part_tiles):
    *x_refs, g_ref, o_ref = refs

    def emit(x_ref):
        x = x_ref[...]
        ms = jnp.mean(x * x, axis=-1, keepdims=True)
        o_ref[...] = (x * lax.rsqrt(ms + EPS) * g_ref[...]).astype(o_ref.dtype)

    first = 0
    for x_ref, n in zip(x_refs, part_tiles):
        i = pl.program_id(0)
        pl.when((i >= first) & (i < first + n))(functools.partial(emit, x_ref))
        first += n


def rmsnorm(x, gain):
    parts = _row_parts(x)
    d = parts[0].shape[1]
    t = sum(p.shape[0] for p in parts)
    tr = _tile(math.gcd(*[p.shape[0] for p in parts]), 512, 8)
    return pl.pallas_call(
        functools.partial(_rmsnorm_kernel, part_tiles=[p.shape[0] // tr for p in parts]),
        out_shape=jax.ShapeDtypeStruct((t, d), BF16),
        grid=(t // tr,),
        in_specs=_row_part_specs(parts, tr, d, lambda i: i, lambda i: 0) + [pl.BlockSpec((1, d), lambda i: (0, 0))],
        out_specs=pl.BlockSpec((tr, d), lambda i: (i, 0)),
        compiler_params=_params("parallel"),
        name="rmsnorm",
    )(*parts, gain.reshape(1, d).astype(F32))


def _lane_partial_sums(v):
    return functools.reduce(lambda a, b: a + b, [v[:, c:c + LANES] for c in range(0, v.shape[1], LANES)])


def _matmul_kernel(a_ref, w_ref, *refs, nk, act, res_tiles, first_tile, chunk, norm_in, norm_out, norm_dim,
                   n_casts):
    refs = list(refs)
    n_out = (3 if norm_out else 1) + n_casts
    outs, refs = refs[len(refs) - n_out:], refs[:len(refs) - n_out]
    o_ref, norm_out_refs, cast_out_refs = outs[0], outs[1:n_out - n_casts], outs[n_out - n_casts:]
    cast_in_refs = [refs.pop() for _ in range(n_casts)][::-1]
    gain_ref = refs.pop() if norm_out else None
    sq_ref = refs.pop() if norm_in else None
    res_refs = refs
    k = pl.program_id(2)

    for src_ref, dst_ref in zip(cast_in_refs, cast_out_refs):
        dst_ref[...] = src_ref[...].astype(dst_ref.dtype)

    def residual(cols):
        return _read_part(res_refs, res_tiles, first_tile + pl.program_id(0), lambda ref: ref[:, cols])

    def emit_norm(x, cols, first):
        xg_ref, sq_out_ref = norm_out_refs
        xg_ref[:, cols] = (x * gain_ref[:, cols]).astype(xg_ref.dtype)
        sq = _lane_partial_sums(x * x)
        sq_out_ref[...] = sq if first else sq_out_ref[...] + sq

    if norm_in:
        scale = lax.rsqrt(jnp.sum(sq_ref[...], axis=-1, keepdims=True) * (1.0 / norm_dim) + EPS)

    if nk > 1:
        @pl.when(k == 0)
        def _():
            o_ref[...] = residual(slice(None)) if res_refs else jnp.zeros_like(o_ref)

    for c in range(o_ref.shape[1] // chunk):
        cols = slice(c * chunk, (c + 1) * chunk)
        part = jnp.dot(a_ref[...], w_ref[:, cols], preferred_element_type=F32)
        if nk > 1:
            part = o_ref[:, cols] + part
            o_ref[:, cols] = part
            if norm_out:
                emit_norm(part, cols, c == 0)
            continue
        if norm_in:
            part = part * scale
        if res_refs:
            part = part + residual(cols)
        if act == "relu2":
            r = jnp.maximum(part, 0.0)
            part = r * r
        o_ref[:, cols] = part.astype(o_ref.dtype)
        if norm_out:
            emit_norm(part, cols, c == 0)


def matmul(a, w, *, out_dtype, layer=None, act=None, res=None, rows=None, norm_in=None, norm_out=None,
           casts=(), tm=1024, tn=1024, tk=2048, chunk=256, name="matmul"):
    row0, m = rows if rows is not None else (0, a.shape[0])
    kdim, n = w.shape[-2:]
    res_parts = _row_parts(res) if res is not None else ()
    tm = _tile(math.gcd(m, row0, *[p.shape[0] for p in res_parts[:-1]]), tm, 8)
    tn, tk = _tile(n, tn, LANES), _tile(kdim, tk, LANES)
    cast_rows_of = []
    for src, _ in casts:
        cast_rows = CAST_ROWS
        while (src.shape[-2] // cast_rows) * (src.shape[-1] // CAST_COLS) > (m // tm) * (n // tn):
            cast_rows *= 2
        cast_rows_of.append(cast_rows)

    def window_bytes(tk):
        per_step = tm * tk * 2 + tk * tn * 2 + tm * tn * (4 * len(res_parts) + jnp.dtype(out_dtype).itemsize)
        if norm_out is not None:
            per_step += tm * tn * 2 + tm * LANES * 4
        if norm_in is not None:
            per_step += tm * norm_in[0].shape[1] * 4
        per_step += sum(cast_rows_of) * CAST_COLS * (4 + 2)
        return 2 * per_step

    while window_bytes(tk) > VMEM_LIMIT_BYTES - MATMUL_TEMP_BYTES and tk % (2 * LANES) == 0:
        tk //= 2
    nk = kdim // tk
    t0 = row0 // tm
    chunk = min(chunk, tn)
    assert nk == 1 or (out_dtype == F32 and act is None), "a split K accumulates in the f32 output block"
    if w.ndim == 3:
        w_spec = pl.BlockSpec((None, tk, tn), lambda i, j, k: (layer, k, j))
    else:
        w_spec = pl.BlockSpec((tk, tn), lambda i, j, k: (k, j))
    in_specs = [pl.BlockSpec((tm, tk), lambda i, j, k: (t0 + i, k)), w_spec]
    args = [a, w]
    in_specs += _row_part_specs(res_parts, tm, tn, lambda i, j, k: t0 + i, lambda i, j, k: j)
    args += res_parts
    norm_dim = None
    if norm_in is not None:
        sq, norm_dim = norm_in
        assert nk == 1, "the row scale is applied where the whole product is formed"
        in_specs.append(pl.BlockSpec((tm, sq.shape[1]), lambda i, j, k: (t0 + i, 0)))
        args.append(sq)
    out_shape = jax.ShapeDtypeStruct((m, n), out_dtype)
    out_specs = pl.BlockSpec((tm, tn), lambda i, j, k: (i, j))
    if norm_out is not None:
        assert out_dtype == F32
        in_specs.append(pl.BlockSpec((1, tn), lambda i, j, k: (0, j)))
        args.append(norm_out.reshape(1, n).astype(F32))
        out_shape = (out_shape, jax.ShapeDtypeStruct((m, n), BF16),
                     jax.ShapeDtypeStruct((m, n // tn * LANES), F32))
        out_specs = (out_specs, pl.BlockSpec((tm, tn), lambda i, j, k: (i, j)),
                     pl.BlockSpec((tm, LANES), lambda i, j, k: (i, j)))
    grid = (m // tm, n // tn, nk)
    for (src, src_layer), cast_rows in zip(casts, cast_rows_of):
        rows_, cols_ = src.shape[-2] // cast_rows, src.shape[-1] // CAST_COLS

        def block(i, j, k, cols_=cols_, last=rows_ * cols_ - 1):
            step = jnp.minimum((i * grid[1] + j) * grid[2] + k, last)
            return step // cols_, step % cols_

        in_specs.append(pl.BlockSpec((None, cast_rows, CAST_COLS),
                                     lambda i, j, k, block=block, src_layer=src_layer: (src_layer, *block(i, j, k))))
        args.append(src)
        out_shape = (*_row_parts(out_shape), jax.ShapeDtypeStruct(src.shape[-2:], BF16))
        out_specs = (*_row_parts(out_specs), pl.BlockSpec((cast_rows, CAST_COLS), block))
    return pl.pallas_call(
        functools.partial(_matmul_kernel, nk=nk, act=act, chunk=chunk, first_tile=t0,
                          res_tiles=[p.shape[0] // tm for p in res_parts], n_casts=len(casts),
                          norm_in=norm_in is not None, norm_out=norm_out is not None, norm_dim=norm_dim),
        out_shape=out_shape,
        grid=grid,
        in_specs=in_specs,
        out_specs=out_specs,
        compiler_params=_params("parallel", "parallel", "arbitrary"),
        name=name,
    )(*args)


def rope_tables(pos, dim):
    inv = ROPE_THETA ** (-jnp.arange(0, dim, 2, dtype=F32) / dim)
    ang = pos.astype(F32)[:, None] * inv[None, :]
    ang = jnp.concatenate([ang, ang], axis=-1)
    return jnp.cos(ang), jnp.sin(ang)


def full_rope_tables(s):
    cos, sin = rope_tables(jnp.arange(s), HEAD_DIM)
    lane = jnp.arange(HEAD_DIM)
    sin_signed = jnp.where(lane < HEAD_DIM // 2, -sin, sin)
    return ((HEAD_DIM // 2,), (cos, sin_signed))


def axial_rope_tables(s):
    half = HEAD_DIM // 2
    t = jnp.arange(s)
    cr, sr = rope_tables(t // GRID_W, half)
    cc, sc = rope_tables(t % GRID_W, half)
    cos = jnp.concatenate([cr, cc], axis=-1)
    sin = jnp.concatenate([sr, sc], axis=-1)
    low = (jnp.arange(HEAD_DIM) % half) < half // 2
    return ((HEAD_DIM - half // 2, half // 2),
            (cos, jnp.where(low, -sin, 0.0), jnp.where(low, 0.0, sin)))


def _prep_kernel(x_ref, gain_ref, *refs, shifts):
    *tab_refs, o_ref = refs
    tables = [t[...] for t in tab_refs]
    averager = jnp.full((HEAD_DIM, HEAD_DIM), 1.0 / HEAD_DIM, BF16)
    for h0 in range(0, x_ref.shape[1], HEAD_DIM):
        head = slice(h0, h0 + HEAD_DIM)
        x = x_ref[:, head].astype(F32)
        ms = jnp.dot((x * x).astype(BF16), averager, preferred_element_type=F32)
        y = x * lax.rsqrt(ms + EPS) * gain_ref[:, head]
        if shifts:
            r = y * tables[0]
            for sh, t in zip(shifts, tables[1:]):
                r = r + pltpu.roll(y, sh, 1) * t
            y = r
        o_ref[:, head] = y.astype(o_ref.dtype)


def in_projection(normed, w_in, layer, segments, rope, seq, tiles_per_group, v_tile, *, tn, name, casts=()):
    gains = jnp.concatenate([jnp.tile(jnp.ones((HEAD_DIM,), F32) if g is None else g.astype(F32), n)
                             for g, n in segments])[None]
    shifts, tables = rope if rope is not None else ((), ())
    a, norm_in = normed
    proj, *cast_weights = _row_parts(matmul(a, w_in, layer=layer, out_dtype=BF16, norm_in=norm_in, casts=casts,
                                            tk=a.shape[1], name=name))
    t, n = proj.shape
    ts = _tile(seq, 1024, 16)
    per_seq = seq // ts
    qk_tiles = n // tn // tiles_per_group * v_tile
    tile = lambda j: j // v_tile * tiles_per_group + j % v_tile
    prepped = pl.pallas_call(
        functools.partial(_prep_kernel, shifts=shifts),
        out_shape=jax.ShapeDtypeStruct((t, n), BF16),
        grid=(t // ts, qk_tiles),
        in_specs=[pl.BlockSpec((ts, tn), lambda i, j: (i, tile(j))), pl.BlockSpec((1, tn), lambda i, j: (0, tile(j)))]
        + [pl.BlockSpec((ts, HEAD_DIM), lambda i, j: (i % per_seq, 0))] * len(tables),
        out_specs=pl.BlockSpec((ts, tn), lambda i, j: (i, tile(j))),
        input_output_aliases={0: 0},
        compiler_params=_params("parallel", "parallel"),
        name=name + "_prep",
    )(proj, gains, *[tab.astype(F32) for tab in tables])
    return (prepped, cast_weights) if casts else prepped


def _stack_heads(x, n):
    return jnp.concatenate([x[:, h * HEAD_DIM:(h + 1) * HEAD_DIM] for h in range(n)], axis=0)


def _unstack_heads(x, n):
    rows = x.shape[0] // n
    return jnp.concatenate([x[h * rows:(h + 1) * rows] for h in range(n)], axis=1)


def _qk(q, k):
    return lax.dot_general(q, k, (((1,), (1,)), ((), ())), preferred_element_type=F32)


def _with_ones(v):
    return jnp.concatenate([v, jnp.ones_like(v)], axis=1)


def _normalised_pv(p, v_ones):
    r = jnp.dot(p, v_ones, preferred_element_type=F32)
    l = r[:, HEAD_DIM:]
    return r[:, :HEAD_DIM] / l, l


def _dense_gqa_kernel(q_ref, k_ref, v_ref, o_ref, *, group, chains):
    k, v = k_ref[0], _with_ones(v_ref[0])
    per = group // chains

    def scores(c):
        return _qk(_stack_heads(q_ref[0, :, c * per * HEAD_DIM:(c + 1) * per * HEAD_DIM], per), k)

    ahead = [scores(c) for c in range(min(2, chains))]
    for c in range(chains):
        cols = slice(c * per * HEAD_DIM, (c + 1) * per * HEAD_DIM)
        s = ahead.pop(0)
        if c + 2 < chains:
            ahead.append(scores(c + 2))
        m = jnp.max(s, axis=-1, keepdims=True)
        o, _ = _normalised_pv(jnp.exp2(s - m).astype(BF16), v)
        o_ref[0, :, cols] = _unstack_heads(o, per).astype(o_ref.dtype)


def dense_gqa(proj, *, kv_heads, group, tq=1024, chains=4):
    b, s, _ = proj.shape
    tq = _tile(s, tq, 16)
    w = group * HEAD_DIM
    k_block0 = kv_heads * group
    v_block0 = k_block0 + kv_heads
    return pl.pallas_call(
        functools.partial(_dense_gqa_kernel, group=group, chains=chains),
        out_shape=jax.ShapeDtypeStruct((b, s, kv_heads * w), BF16),
        grid=(b, kv_heads, s // tq),
        in_specs=[
            pl.BlockSpec((1, tq, w), lambda bi, g, i: (bi, i, g)),
            pl.BlockSpec((1, s, HEAD_DIM), lambda bi, g, i: (bi, 0, k_block0 + g)),
            pl.BlockSpec((1, s, HEAD_DIM), lambda bi, g, i: (bi, 0, v_block0 + g)),
        ],
        out_specs=pl.BlockSpec((1, tq, w), lambda bi, g, i: (bi, i, g)),
        compiler_params=_params("parallel", "parallel", "arbitrary"),
        name="dense_gqa",
    )(proj, proj, proj)


def _dilated_kernel(*refs, configs, seq, group):
    o_ref = refs[-1]
    i = pl.program_id(2)
    tq = o_ref.shape[1]

    chains = []
    for gi, (window, dil) in enumerate(configs):
        reach = (window // (2 * dil)) * dil
        pad = -(-reach // tq) * tq
        wlen = min(seq, tq + 2 * pad)
        start = pl.multiple_of(jnp.clip(i * tq - pad, 0, seq - wlen), tq)
        u = (start + lax.broadcasted_iota(jnp.int32, (tq, wlen), 1)
             - i * tq - lax.broadcasted_iota(jnp.int32, (tq, wlen), 0) + reach)
        if dil > 1:
            u = u + ((u & (dil - 1)) << RESIDUE_LIFT_BITS)
        ok = lax.bitcast_convert_type(u, jnp.uint32) <= jnp.uint32(2 * reach)
        bias = jnp.where(ok, 0.0, MASKED)
        per = group if wlen <= 4 * tq else max(1, group // 4)
        for h0 in range(0, group, per):
            chains.append((gi, h0, per, start, wlen, bias))

    def scores(c):
        gi, h0, per, start, wlen, _ = chains[c]
        q_ref, k_ref = refs[3 * gi], refs[3 * gi + 1]
        q = _stack_heads(q_ref[0, :, h0 * HEAD_DIM:(h0 + per) * HEAD_DIM], per)
        return _qk(q, k_ref[0, pl.ds(start, wlen), :])

    outs = [[None] * group for _ in configs]
    lses = [[None] * group for _ in configs]
    ahead = [scores(c) for c in range(min(2, len(chains)))]
    for c, (gi, h0, per, start, wlen, bias) in enumerate(chains):
        s = ahead.pop(0)
        if c + 2 < len(chains):
            ahead.append(scores(c + 2))
        s = (s.reshape(per, tq, wlen) + bias[None]).reshape(per * tq, wlen)
        m = jnp.max(s, axis=-1, keepdims=True)
        v_ref = refs[3 * gi + 2]
        o, l = _normalised_pv(jnp.exp2(s - m).astype(BF16), _with_ones(v_ref[0, pl.ds(start, wlen), :]))
        lse = m + jnp.log2(l[:, :1])
        for h in range(per):
            outs[gi][h0 + h] = o[h * tq:(h + 1) * tq]
            lses[gi][h0 + h] = lse[h * tq:(h + 1) * tq]

    for h in range(group):
        lse_h = [lses[gi][h] for gi in range(len(configs))]
        top = functools.reduce(jnp.maximum, lse_h)
        ws = [jnp.exp2(x - top) for x in lse_h]
        o = sum(w * outs[gi][h] for gi, w in enumerate(ws)) / sum(ws)
        o_ref[0, :, h * HEAD_DIM:(h + 1) * HEAD_DIM] = o.astype(o_ref.dtype)


def dilated_attention(proj, *, configs=B_CONFIGS, tq=256):
    b, s, _ = proj.shape
    tq = _tile(s, tq, LANES)
    for window, dil in configs:
        assert dil & (dil - 1) == 0, "the residue test uses a power-of-two dilation"
        assert 2 * window + s < 1 << RESIDUE_LIFT_BITS, "a lifted residue must clear every in-range offset"
    group = B_HEADS // B_KV_HEADS
    w = group * HEAD_DIM
    per_group_blocks = B_HEADS + 2 * B_KV_HEADS
    in_specs, args = [], []
    for gi in range(len(configs)):
        q0 = gi * per_group_blocks // group
        k0 = gi * per_group_blocks + B_HEADS
        in_specs += [
            pl.BlockSpec((1, tq, w), lambda bi, h, i, q0=q0: (bi, i, q0 + h)),
            pl.BlockSpec((1, s, HEAD_DIM), lambda bi, h, i, k0=k0: (bi, 0, k0 + h)),
            pl.BlockSpec((1, s, HEAD_DIM), lambda bi, h, i, k0=k0: (bi, 0, k0 + B_KV_HEADS + h)),
        ]
        args += [proj, proj, proj]
    return pl.pallas_call(
        functools.partial(_dilated_kernel, configs=configs, seq=s, group=group),
        out_shape=jax.ShapeDtypeStruct((b, s, B_HEADS * HEAD_DIM), BF16),
        grid=(b, B_KV_HEADS, s // tq),
        in_specs=in_specs,
        out_specs=pl.BlockSpec((1, tq, w), lambda bi, h, i: (bi, i, h)),
        compiler_params=_params("parallel", "parallel", "arbitrary"),
        name="dilated_attention",
    )(*args)


def _differential_kernel(q_ref, k_ref, v_ref, lam_ref, sub_ref, o_ref, *, lambda_init, chains):
    lp = lam_ref[...]
    lam = (jnp.exp(jnp.sum(lp[0:1] * lp[1:2], axis=-1, keepdims=True))
           - jnp.exp(jnp.sum(lp[2:3] * lp[3:4], axis=-1, keepdims=True)) + lambda_init)
    k, v = k_ref[0], v_ref[0]
    rows = q_ref.shape[1] // chains
    def scores(c):
        q = q_ref[0, c * rows:(c + 1) * rows, :]
        return [_qk(q[:, mi * HEAD_DIM:(mi + 1) * HEAD_DIM], k[:, mi * HEAD_DIM:(mi + 1) * HEAD_DIM])
                for mi in range(2)]

    nxt = scores(0)
    for c in range(chains):
        cur = nxt
        if c + 1 < chains:
            nxt = scores(c + 1)
        probs = []
        for s in cur:
            m = jnp.max(s, axis=-1, keepdims=True)
            p = jnp.exp2(s - m)
            probs.append((p, jnp.sum(p, axis=-1, keepdims=True)))
        (p0, l0), (p1, l1) = probs
        a = p0 - p1 * (lam * l0 / l1)
        o = jnp.dot(a.astype(BF16), v, preferred_element_type=F32) / l0
        ms = jnp.mean(o * o, axis=-1, keepdims=True)
        o = o * lax.rsqrt(ms + EPS) * sub_ref[...] * (1.0 - lambda_init)
        o_ref[0, c * rows:(c + 1) * rows, :] = o.astype(o_ref.dtype)


def differential_attention(proj, lam_params, subln, lambda_init, *, tq=2048, chains=8):
    b, s, _ = proj.shape
    tq = _tile(s, tq, 16)
    w = 2 * HEAD_DIM
    return pl.pallas_call(
        functools.partial(_differential_kernel, lambda_init=lambda_init, chains=chains),
        out_shape=jax.ShapeDtypeStruct((b, s, C_HEADS * w), BF16),
        grid=(b, C_HEADS, s // tq),
        in_specs=[
            pl.BlockSpec((1, tq, w), lambda bi, h, i: (bi, i, h)),
            pl.BlockSpec((1, s, w), lambda bi, h, i: (bi, 0, C_HEADS + h)),
            pl.BlockSpec((1, s, w), lambda bi, h, i: (bi, 0, 2 * C_HEADS + h)),
            pl.BlockSpec((4, HEAD_DIM), lambda bi, h, i: (0, 0)),
            pl.BlockSpec((1, w), lambda bi, h, i: (0, 0)),
        ],
        out_specs=pl.BlockSpec((1, tq, w), lambda bi, h, i: (bi, i, h)),
        compiler_params=_params("parallel", "parallel", "arbitrary"),
        name="differential_attention",
    )(proj, proj, proj, lam_params.astype(F32), subln.reshape(1, w).astype(F32))


def _na_bias_kernel(rpb_ref, o_ref):
    rows = lax.broadcasted_iota(jnp.int32, (GRID_W, LANES), 0)
    lane = lax.broadcasted_iota(jnp.int32, (GRID_W, LANES), 1)
    n_rel = 2 * NA_ROWS - 1
    toeplitz = []
    for rho in range(n_rel):
        z = jnp.broadcast_to(rpb_ref[0, rho:rho + 1, :], (GRID_W, LANES))
        z = pltpu.roll(z, LANES - (NA_COLS - 1), 1)
        bit = 1
        while bit < GRID_W:
            z = jnp.where((rows & bit) != 0, pltpu.roll(z, bit, 1), z)
            bit *= 2
        toeplitz.append(z)
    kc = lane & (GRID_W - 1)
    first = jnp.clip(rows - NA_COLS // 2, 0, GRID_W - NA_COLS)
    inside = (kc >= first) & (kc < first + NA_COLS)
    for d in range(NA_ROWS):
        for pair in range(NA_ROWS // 2):
            even = toeplitz[d + 2 * pair]
            odd = pltpu.roll(toeplitz[d + 2 * pair + 1], GRID_W, 1)
            tile = jnp.where(lane < GRID_W, even, odd)
            o_ref[d, 0, :, pair * LANES:(pair + 1) * LANES] = jnp.where(inside, tile * LOG2E, MASKED)


def na_bias_table(rpb):
    h = rpb.shape[0]
    padded = jnp.zeros((h, 2 * NA_ROWS, LANES), F32).at[:, :2 * NA_ROWS - 1, :2 * NA_COLS - 1].set(rpb.astype(F32))
    return pl.pallas_call(
        _na_bias_kernel,
        out_shape=jax.ShapeDtypeStruct((NA_ROWS, h, GRID_W, NA_ROWS * GRID_W), F32),
        grid=(h,),
        in_specs=[pl.BlockSpec((1, 2 * NA_ROWS, LANES), lambda i: (i, 0, 0))],
        out_specs=pl.BlockSpec((NA_ROWS, 1, GRID_W, NA_ROWS * GRID_W), lambda i: (0, i, 0, 0)),
        compiler_params=_params("parallel"),
        name="na_bias_table",
    )(padded)


def _neighbourhood_kernel(q_ref, k_ref, v_ref, b_ref, o_ref, *, n_rows, group):
    win = NA_ROWS * GRID_W

    def row(r, carry):
        first = jnp.clip(r - NA_ROWS // 2, 0, n_rows - NA_ROWS)
        q0 = pl.multiple_of(r * GRID_W, GRID_W)
        k0 = pl.multiple_of(first * GRID_W, GRID_W)
        q = _stack_heads(q_ref[0, pl.ds(q0, GRID_W), :], group)
        s = _qk(q, k_ref[0, pl.ds(k0, win), :])
        s = (s.reshape(group, GRID_W, win) + b_ref[first - r + NA_ROWS - 1]).reshape(group * GRID_W, win)
        m = jnp.max(s, axis=-1, keepdims=True)
        o, _ = _normalised_pv(jnp.exp2(s - m).astype(BF16), _with_ones(v_ref[0, pl.ds(k0, win), :]))
        o_ref[0, pl.ds(q0, GRID_W), :] = _unstack_heads(o, group).astype(o_ref.dtype)
        return carry

    lax.fori_loop(0, n_rows, row, 0, unroll=32)


def neighbourhood_attention(proj, bias, *, kv_heads, group):
    b, s, _ = proj.shape
    n_rows = s // GRID_W
    assert n_rows >= NA_ROWS, "the key window is NA_ROWS grid rows tall"
    w = group * HEAD_DIM
    k_block0 = kv_heads * group
    v_block0 = k_block0 + kv_heads
    return pl.pallas_call(
        functools.partial(_neighbourhood_kernel, n_rows=n_rows, group=group),
        out_shape=jax.ShapeDtypeStruct((b, s, kv_heads * w), BF16),
        grid=(b, kv_heads),
        in_specs=[
            pl.BlockSpec((1, s, w), lambda bi, g: (bi, 0, g)),
            pl.BlockSpec((1, s, HEAD_DIM), lambda bi, g: (bi, 0, k_block0 + g)),
            pl.BlockSpec((1, s, HEAD_DIM), lambda bi, g: (bi, 0, v_block0 + g)),
            pl.BlockSpec((NA_ROWS, group, GRID_W, NA_ROWS * GRID_W), lambda bi, g: (0, g, 0, 0)),
        ],
        out_specs=pl.BlockSpec((1, s, w), lambda bi, g: (bi, 0, g)),
        compiler_params=_params("parallel", "arbitrary"),
        name="neighbourhood_attention",
    )(proj, proj, proj, bias)


IN_TILE = 1024


def _with_next_norm(outputs):
    x, xg, sq = outputs
    return x, (xg, (sq, x.shape[1]))


def _out_projection(o, x, w_out, layer, next_gain, name):
    tn = 1024 // len(_row_parts(x))
    return _with_next_norm(matmul(o.reshape(-1, o.shape[-1]), w_out, layer=layer, out_dtype=F32, res=x,
                                  norm_out=next_gain, tk=o.shape[-1], tn=tn, name=name))


def mixer_axial_gqa(normed, x, w_in, w_out, layer, next_gain, q_gain, k_gain, b, s, casts=()):
    segments = [(q_gain * Q_SCALE, A_HEADS), (k_gain, A_KV_HEADS), (None, A_KV_HEADS)]
    tiles = (A_HEADS + 2 * A_KV_HEADS) * HEAD_DIM // IN_TILE
    proj, *cast_weights = _row_parts(in_projection(
        normed, w_in, layer, segments, axial_rope_tables(s), s, tiles,
        (A_HEADS + A_KV_HEADS) * HEAD_DIM // IN_TILE, tn=IN_TILE, name="a_in", casts=casts))
    o = dense_gqa(proj.reshape(b, s, -1), kv_heads=A_KV_HEADS, group=A_HEADS // A_KV_HEADS)
    return (*_out_projection(o, x, w_out, layer, next_gain, "a_out"), *cast_weights)


def mixer_dilated(normed, x, w_in, w_out, layer, next_gain, q_gain, k_gain, b, s):
    segments = []
    for gi in range(len(B_CONFIGS)):
        segments += [(q_gain[gi] * Q_SCALE, B_HEADS), (k_gain[gi], B_KV_HEADS), (None, B_KV_HEADS)]
    tn = B_KV_HEADS * HEAD_DIM
    tiles = (B_HEADS + 2 * B_KV_HEADS) * HEAD_DIM // tn
    proj = in_projection(normed, w_in, layer, segments, full_rope_tables(s), s, tiles, tiles - 1, tn=tn, name="b_in")
    o = dilated_attention(proj.reshape(b, s, -1))
    return _out_projection(o, x, w_out, layer, next_gain, "b_out")


def mixer_differential(normed, x, w_in, w_out, layer, next_gain, q_gain, k_gain, lam_params, subln, lambda_init,
                       b, s):
    segments = [(q_gain * Q_SCALE, 2 * C_HEADS), (k_gain, 2 * C_HEADS), (None, 2 * C_HEADS)]
    tiles = 3 * 2 * C_HEADS * HEAD_DIM // IN_TILE
    proj = in_projection(normed, w_in, layer, segments, full_rope_tables(s), s, tiles, 2 * tiles // 3,
                         tn=IN_TILE, name="c_in")
    o = differential_attention(proj.reshape(b, s, -1), lam_params, subln, lambda_init)
    return _out_projection(o, x, w_out, layer, next_gain, "c_out")


def mixer_neighbourhood(normed, x, w_in, w_out, layer, next_gain, q_gain, k_gain, rpb, b, s):
    segments = [(q_gain * Q_SCALE, D_HEADS), (k_gain, D_KV_HEADS), (None, D_KV_HEADS)]
    tiles = (D_HEADS + 2 * D_KV_HEADS) * HEAD_DIM // IN_TILE
    proj = in_projection(normed, w_in, layer, segments, None, s, tiles,
                         (D_HEADS + D_KV_HEADS) * HEAD_DIM // IN_TILE, tn=IN_TILE, name="d_in")
    o = neighbourhood_attention(proj.reshape(b, s, -1), na_bias_table(rpb),
                                kv_heads=D_KV_HEADS, group=D_HEADS // D_KV_HEADS)
    return _out_projection(o, x, w_out, layer, next_gain, "d_out")


def squared_relu_mlp(normed, x, w_up, w_down, next_gain=None, next_weights=(), split=None):
    a, norm_in = normed
    u, *cast_weights = _row_parts(matmul(a, w_up, out_dtype=BF16, act="relu2", norm_in=norm_in, casts=next_weights,
                                         tk=a.shape[1], name="mlp_up"))
    down = functools.partial(matmul, u, w_down, out_dtype=F32, res=x, tk=4096, name="mlp_down")
    if split is None:
        return (*_with_next_norm(down(norm_out=next_gain)), cast_weights)
    return down(rows=(0, split)), down(rows=(split, u.shape[0] - split))


def trunk(x, b, s, ln_mix, ln_ffn, w_up, w_down, mixers, split):
    depth = ln_mix.shape[0]
    normed = (rmsnorm(x, ln_mix[0]), None)
    mlp_weights = None
    mixer_weights = (mixers[0]["w_in"].astype(BF16), mixers[0]["w_out"].astype(BF16))
    for i in range(depth):
        m, j = i % N_MIXERS, i // N_MIXERS
        p = mixers[m]
        layer = j if mixer_weights[0].ndim == 3 else None
        common = (normed, x, *mixer_weights, layer, ln_ffn[i],
                  p["q_gain"][j].astype(F32), p["k_gain"][j].astype(F32))
        if m == 0 and i == 0:
            x, normed, mlp_weights = mixer_axial_gqa(*common, b, s, casts=[(w_up, 0), (w_down, 0)])
        elif m == 0:
            x, normed = mixer_axial_gqa(*common, b, s)
        elif m == 1:
            x, normed = mixer_dilated(*common, b, s)
        elif m == 2:
            lambda_init = 0.8 - 0.6 * math.exp(-0.3 * i)
            x, normed = mixer_differential(*common, p["lam"][j], p["subln"][j], lambda_init, b, s)
        else:
            x, normed = mixer_neighbourhood(*common, p["rpb"][j], b, s)
        if mlp_weights is None:
            mlp_weights = [w_up[i].astype(BF16), w_down[i].astype(BF16)]
        if i == depth - 1:
            return squared_relu_mlp(normed, x, *mlp_weights, split=split)
        nxt, nxt_j = mixers[(i + 1) % N_MIXERS], (i + 1) // N_MIXERS
        next_weights = [(w_up, i + 1), (w_down, i + 1)]
        if i > 0:
            next_weights += [(nxt["w_in"], nxt_j), (nxt["w_out"], nxt_j)]
        x, normed, cast_weights = squared_relu_mlp(normed, x, *mlp_weights, next_gain=ln_mix[i + 1],
                                                  next_weights=next_weights)
        mlp_weights, mixer_weights = cast_weights[:2], tuple(cast_weights[2:])
        if not mixer_weights:
            mixer_weights = (nxt["w_in"][nxt_j].astype(BF16), nxt["w_out"][nxt_j].astype(BF16))


def kernel(x_prompt, x_sample, ln_mix, ln_ffn, w_up, w_down, a_w_in, a_w_out, a_q_gain, a_k_gain, b_w_in, b_w_out, b_q_gain, b_k_gain, c_w_in, c_w_out, c_q_gain, c_k_gain, c_lambda, c_subln, d_w_in, d_w_out, d_q_gain, d_k_gain, d_rpb):
    bp, s, d = x_prompt.shape
    bs = x_sample.shape[0]
    assert x_sample.shape[1:] == (s, d)
    b = bp + bs
    x = (x_prompt.reshape(bp * s, d), x_sample.reshape(bs * s, d))
    mixers = [
        dict(w_in=a_w_in, w_out=a_w_out, q_gain=a_q_gain, k_gain=a_k_gain),
        dict(w_in=b_w_in, w_out=b_w_out, q_gain=b_q_gain, k_gain=b_k_gain),
        dict(w_in=c_w_in, w_out=c_w_out, q_gain=c_q_gain, k_gain=c_k_gain, lam=c_lambda, subln=c_subln),
        dict(w_in=d_w_in, w_out=d_w_out, q_gain=d_q_gain, k_gain=d_k_gain, rpb=d_rpb),
    ]
    y_prompt, y_sample = trunk(x, b, s, ln_mix, ln_ffn, w_up, w_down, mixers, split=bp * s)
    return (y_prompt.reshape(bp, s, d), y_sample.reshape(bs, s, d))
```
